```python
import math
import jax, jax.numpy as jnp
from jax import lax
import numpy as np

D_MODEL = 1024
BATCH = 4
SEQ = 4096
DEPTH = 2

N_HEADS = 16
HEAD_DIM = 64
MIX_WIDTH = N_HEADS * HEAD_DIM
N_MIXERS = 2
SB_Q_BLOCK = 128
MOBA_BLOCK = 256
MOBA_TOPK = 3
MOBA_Q_CHUNK = 32
REL_BUCKETS = 32
REL_MAX_DIST = 128
LN_EPS = 1e-5
DEEPNORM_ALPHA = (2.0 * DEPTH) ** 0.25
DEEPNORM_BETA = (8.0 * DEPTH) ** -0.25
NEG_INF = -1e30

kernel_name = "hybrid_stickbreak_moba_deepnorm"


def layer_norm(x, g, b):
    xf = x.astype(jnp.float32)
    mu = jnp.mean(xf, axis=-1, keepdims=True)
    var = jnp.mean(jnp.square(xf - mu), axis=-1, keepdims=True)
    y = (xf - mu) * lax.rsqrt(var + LN_EPS) * g.astype(jnp.float32) + b.astype(jnp.float32)
    return y.astype(x.dtype)


def t5_bucket(dist):
    n = jnp.maximum(dist, 0)
    max_exact = REL_BUCKETS // 2
    nf = jnp.maximum(n, 1).astype(jnp.float32)
    large = max_exact + (jnp.log(nf / max_exact) / math.log(REL_MAX_DIST / max_exact)
                         * (REL_BUCKETS - max_exact)).astype(jnp.int32)
    large = jnp.minimum(large, REL_BUCKETS - 1)
    return jnp.where(n < max_exact, n, large)


def split_heads(t):
    B, S, _ = t.shape
    return t.reshape(B, S, N_HEADS, HEAD_DIM).transpose(0, 2, 1, 3)


def stick_breaking_attention(q, k, v):
    S = q.shape[2]
    scale = HEAD_DIM ** -0.5
    outs = []
    for blk in range(S // SB_Q_BLOCK):
        q0 = blk * SB_Q_BLOCK
        q1 = q0 + SB_Q_BLOCK
        qb = q[:, :, q0:q1]
        kb = k[:, :, :q1]
        vb = v[:, :, :q1]
        z = jnp.einsum('bhqd,bhkd->bhqk', qb, kb).astype(jnp.float32) * scale
        past = jnp.arange(q1)[None, :] < jnp.arange(q0, q1)[:, None]
        log_1mb = jnp.where(past, jax.nn.log_sigmoid(-z), 0.0)
        rev_cs = lax.cumsum(log_1mb, axis=log_1mb.ndim - 1, reverse=True)
        log_w = jax.nn.log_sigmoid(z) + rev_cs - log_1mb
        w = jnp.where(past, jnp.exp(log_w), 0.0)
        outs.append(jnp.einsum('bhqk,bhkd->bhqd', w.astype(v.dtype), vb))
    return jnp.concatenate(outs, axis=2)


def moba_attention(q, k, v, rel_table):
    B, H, S, Dh = q.shape
    scale = Dh ** -0.5
    nb = -(-S // MOBA_BLOCK)
    pad = nb * MOBA_BLOCK - S
    kp = jnp.pad(k, ((0, 0), (0, 0), (0, pad), (0, 0)))
    vp = jnp.pad(v, ((0, 0), (0, 0), (0, pad), (0, 0)))
    k_blocks = kp.reshape(B, H, nb, MOBA_BLOCK, Dh)
    v_blocks = vp.reshape(B, H, nb, MOBA_BLOCK, Dh)
    k_mean = jnp.mean(k_blocks.astype(jnp.float32), axis=3)

    pos = jnp.arange(S, dtype=jnp.int32)
    own = pos // MOBA_BLOCK
    gate = jnp.einsum('bhsd,bhnd->bhsn', q.astype(jnp.float32), k_mean)
    past_block = jnp.arange(nb)[None, :] < own[:, None]
    gate = jnp.where(past_block, gate, NEG_INF)
    kk = min(MOBA_TOPK, nb)
    _, top_idx = lax.top_k(gate, kk)
    top_ok = jnp.arange(kk)[None, :] < jnp.minimum(own, MOBA_TOPK)[:, None]
    own_idx = jnp.broadcast_to(own[None, None, :, None], (B, H, S, 1))
    sel_idx = jnp.concatenate([top_idx.astype(jnp.int32), own_idx], axis=-1)
    slot_ok = jnp.concatenate([top_ok, jnp.ones((S, 1), dtype=bool)], axis=-1)
    J = kk + 1

    nc = S // MOBA_Q_CHUNK
    C = MOBA_Q_CHUNK
    q_c = jnp.moveaxis(q.reshape(B, H, nc, C, Dh), 2, 0)
    idx_c = jnp.moveaxis(sel_idx.reshape(B, H, nc, C, J), 2, 0)
    ok_c = slot_ok.reshape(nc, C, J)
    qpos_c = pos.reshape(nc, C)
    bi = jnp.arange(B)[:, None, None, None]
    hi = jnp.arange(H)[None, :, None, None]
    hi5 = jnp.arange(H)[None, :, None, None, None]
    offs = jnp.arange(MOBA_BLOCK, dtype=jnp.int32)

    def chunk_fn(args):
        qc, idxc, okc, qposc = args
        kg = k_blocks[bi, hi, idxc]
        vg = v_blocks[bi, hi, idxc]
        kpos = idxc[..., None] * MOBA_BLOCK + offs
        dist = qposc[None, None, :, None, None] - kpos
        valid = okc[None, None, :, :, None] & (dist >= 0)
        bias = rel_table[t5_bucket(dist), hi5].astype(jnp.float32)
        logits = jnp.einsum('bhcd,bhcjld->bhcjl', qc, kg).astype(jnp.float32) * scale + bias
        logits = jnp.where(valid, logits, NEG_INF).reshape(B, H, C, J * MOBA_BLOCK)
        p = jax.nn.softmax(logits, axis=-1).reshape(B, H, C, J, MOBA_BLOCK)
        return jnp.einsum('bhcjl,bhcjld->bhcd', p.astype(vg.dtype), vg)

    out = lax.map(chunk_fn, (q_c, idx_c, ok_c, qpos_c))
    return jnp.moveaxis(out, 0, 2).reshape(B, H, S, Dh)


def hybrid_layer(x, w_in, w_out, ln_g, ln_b, rel_table, mixer_id):
    B, S, _ = x.shape
    proj = jnp.einsum('bsd,de->bse', x, w_in)
    q, k, v, g = jnp.split(proj, 4, axis=-1)
    q, k, v = split_heads(q), split_heads(k), split_heads(v)
    if mixer_id == 0:
        o = stick_breaking_attention(q, k, v)
    else:
        o = moba_attention(q, k, v, rel_table)
    o = o.transpose(0, 2, 1, 3).reshape(B, S, MIX_WIDTH)
    y = jnp.einsum('bse,ed->bsd', o * jax.nn.silu(g), w_out)
    return layer_norm(DEEPNORM_ALPHA * x + y, ln_g, ln_b)


def setup_inputs(seed: int = 0) -> dict:
    key = jax.random.key(seed)
    kx, kin, kout, kg, kb, kr = jax.random.split(key, 6)
    x = jax.random.normal(kx, (BATCH, SEQ, D_MODEL), jnp.float32)
    w_in = jax.random.normal(kin, (DEPTH, D_MODEL, 4 * MIX_WIDTH), jnp.float32) * D_MODEL ** -0.5
    w_out = (jax.random.normal(kout, (DEPTH, MIX_WIDTH, D_MODEL), jnp.float32)
             * MIX_WIDTH ** -0.5 * DEEPNORM_BETA)
    ln_g = 1.0 + 0.02 * jax.random.normal(kg, (DEPTH, D_MODEL), jnp.float32)
    ln_b = 0.02 * jax.random.normal(kb, (DEPTH, D_MODEL), jnp.float32)
    rel_table = 0.2 * jax.random.normal(kr, (REL_BUCKETS, N_HEADS), jnp.float32)
    return {"x": x, "w_in": w_in, "w_out": w_out, "ln_g": ln_g, "ln_b": ln_b,
            "rel_table": rel_table}


def reference(x, w_in, w_out, ln_g, ln_b, rel_table):
    h = x
    for layer in range(DEPTH):
        h = hybrid_layer(h, w_in[layer], w_out[layer], ln_g[layer], ln_b[layer],
                         rel_table, layer % N_MIXERS)
    return h
```

```python
import functools
import math

import jax
import jax.numpy as jnp
import numpy as np
from jax import lax
from jax.experimental import pallas as pl
from jax.experimental.pallas import tpu as pltpu

N_HEADS = 16
HEAD_DIM = 64
DEPTH = 2
MOBA_BLOCK = 256
MOBA_TOPK = 3
REL_BUCKETS = 32
REL_MAX_DIST = 128
LN_EPS = 1e-5
DEEPNORM_ALPHA = (2.0 * DEPTH) ** 0.25
NEG_INF = -1e30

BLK = 256
PROJ_ROWS = 512
VMEM_LIMIT_BYTES = 56 * 1024 * 1024

F32 = jnp.float32
BF16 = jnp.bfloat16


def _proj_kernel(x_ref, wk_ref, wt_ref, k_ref, qT_ref, vT_ref, gT_ref, *, width, scale):
    xb = x_ref[0].astype(BF16)
    k_ref[0] = jnp.dot(xb, wk_ref[...], preferred_element_type=F32).astype(BF16)
    nt = (((1,), (1,)), ((), ()))
    n_blk = xb.shape[0] // BLK

    def t_part(i):
        return lax.dot_general(wt_ref[i * width:(i + 1) * width, :], xb, nt,
                               preferred_element_type=F32)

    qT = (t_part(0) * scale).astype(BF16)
    vT = t_part(1).astype(BF16)
    for j in range(n_blk):
        qT_ref[0, j] = qT[:, j * BLK:(j + 1) * BLK]
        vT_ref[0, j] = vT[:, j * BLK:(j + 1) * BLK]
    gT_ref[0] = t_part(2)


def _project(x, wk, wt, *, width, scale):
    B, S, D = x.shape
    rows = min(PROJ_ROWS, S)
    n_blk = rows // BLK
    nkb = S // BLK
    kern = functools.partial(_proj_kernel, width=width, scale=scale)
    return pl.pallas_call(
        kern,
        grid=(B, S // rows),
        in_specs=[
            pl.BlockSpec((1, rows, D), lambda b, s: (b, s, 0)),
            pl.BlockSpec((D, width), lambda b, s: (0, 0)),
            pl.BlockSpec((3 * width, D), lambda b, s: (0, 0)),
        ],
        out_specs=[
            pl.BlockSpec((1, rows, width), lambda b, s: (b, s, 0)),
            pl.BlockSpec((1, n_blk, width, BLK), lambda b, s: (b, s, 0, 0)),
            pl.BlockSpec((1, n_blk, width, BLK), lambda b, s: (b, s, 0, 0)),
            pl.BlockSpec((1, width, rows), lambda b, s: (b, 0, s)),
        ],
        out_shape=[
            jax.ShapeDtypeStruct((B, S, width), BF16),
            jax.ShapeDtypeStruct((B, nkb, width, BLK), BF16),
            jax.ShapeDtypeStruct((B, nkb, width, BLK), BF16),
            jax.ShapeDtypeStruct((B, width, S), F32),
        ],
        compiler_params=pltpu.CompilerParams(
            dimension_semantics=("arbitrary", "arbitrary"), vmem_limit_bytes=VMEM_LIMIT_BYTES),
        name="proj",
    )(x, wk, wt)


def _pair_queries(qT, par):
    zero = jnp.zeros_like(qT)
    return jnp.concatenate([jnp.where(par == 0, qT, zero), jnp.where(par == 1, qT, zero)], axis=0)


def _attn_specs(width):
    del width
    q_spec = pl.BlockSpec((1, 1, HEAD_DIM, BLK), lambda b, h, q: (b, q, h, 0))
    k_spec = lambda S: pl.BlockSpec((1, S, 2 * HEAD_DIM), lambda b, h, q: (b, 0, h // 2))
    v_spec = lambda nkb: pl.BlockSpec((1, nkb, HEAD_DIM, BLK), lambda b, h, q: (b, 0, h, 0))
    o_spec = pl.BlockSpec((1, HEAD_DIM, BLK), lambda b, h, q: (b, h, q))
    return q_spec, k_spec, v_spec, o_spec


def _sb_kernel(qT_ref, k_ref, vT_ref, u2_ref, oT_ref, acc_ref, r_ref):
    h = pl.program_id(1)
    qi = pl.program_id(2)
    qT2 = _pair_queries(qT_ref[0, 0], h % 2)
    row = lax.broadcasted_iota(jnp.int32, (BLK, BLK), 0)
    col = lax.broadcasted_iota(jnp.int32, (BLK, BLK), 1)
    past = row < col

    acc_ref[...] = jnp.zeros_like(acc_ref)
    r_ref[...] = jnp.zeros_like(r_ref)

    def block(kb, masked):
        start = pl.multiple_of(kb * BLK, BLK)
        z = jnp.dot(k_ref[0, pl.ds(start, BLK), :], qT2, preferred_element_type=F32)
        p = jnp.maximum(z, 0.0) + jnp.log(1.0 + jnp.exp(-jnp.abs(z)))
        if masked:
            p = jnp.where(past, p, 0.0)
        hi = p.astype(BF16)
        lo = (p - hi.astype(F32)).astype(BF16)
        c = jnp.dot(u2_ref[...], jnp.concatenate([hi, lo], axis=0),
                    preferred_element_type=F32) + r_ref[...]
        w = jnp.exp(z - c)
        if masked:
            w = jnp.where(past, w, 0.0)
        r_ref[...] = c[0:1, :]
        acc_ref[...] += jnp.dot(vT_ref[0, kb], w.astype(BF16), preferred_element_type=F32)

    block(qi, True)

    def body(i, carry):
        block(qi - 1 - i, False)
        return carry

    lax.fori_loop(0, qi, body, 0)
    oT_ref[0] = acc_ref[...]


def _stick_breaking(qT, k, vT, u2):
    B, nkb, width, _ = qT.shape
    S = nkb * BLK
    q_spec, k_spec, v_spec, o_spec = _attn_specs(width)
    return pl.pallas_call(
        _sb_kernel,
        grid=(B, N_HEADS, nkb),
        in_specs=[q_spec, k_spec(S), v_spec(nkb),
                  pl.BlockSpec((BLK, 2 * BLK), lambda b, h, q: (0, 0))],
        out_specs=o_spec,
        out_shape=jax.ShapeDtypeStruct((B, width, S), F32),
        scratch_shapes=[pltpu.VMEM((HEAD_DIM, BLK), F32), pltpu.VMEM((1, BLK), F32)],
        compiler_params=pltpu.CompilerParams(
            dimension_semantics=("arbitrary", "arbitrary", "arbitrary"),
            vmem_limit_bytes=VMEM_LIMIT_BYTES),
        name="stickbreak",
    )(qT, k, vT, u2)


def _bias_kernel(table_ref, bucket_ref, bias_ref):
    h = pl.program_id(0)
    for t in range(2):
        bk = bucket_ref[t]
        out = jnp.zeros(bk.shape, F32)
        for i in range(REL_BUCKETS):
            out = jnp.where(bk == i, table_ref[i * N_HEADS + h], out)
        bias_ref[0, t] = out


def _t5_bucket_np(dist):
    n = np.maximum(dist, 0)
    max_exact = REL_BUCKETS // 2
    nf = np.maximum(n, 1).astype(np.float64)
    large = max_exact + (np.log(nf / max_exact) / math.log(REL_MAX_DIST / max_exact)
                         * (REL_BUCKETS - max_exact)).astype(np.int32)
    large = np.minimum(large, REL_BUCKETS - 1)
    return np.where(n < max_exact, n, large).astype(np.int32)


def _bias_tiles(rel_table):
    s = np.arange(BLK)[:, None]
    t = np.arange(BLK)[None, :]
    buckets = np.stack([_t5_bucket_np(t - s), _t5_bucket_np(t - s + BLK)])
    return pl.pallas_call(
        _bias_kernel,
        grid_spec=pltpu.PrefetchScalarGridSpec(
            num_scalar_prefetch=1,
            grid=(N_HEADS,),
            in_specs=[pl.BlockSpec((2, BLK, BLK), lambda h, tab: (0, 0, 0))],
            out_specs=pl.BlockSpec((1, 2, BLK, BLK), lambda h, tab: (h, 0, 0, 0)),
        ),
        out_shape=jax.ShapeDtypeStruct((N_HEADS, 2, BLK, BLK), F32),
        name="moba_bias",
    )(rel_table.reshape(-1), jnp.asarray(buckets))


def _moba_kernel(far_ref, qT_ref, k_ref, vT_ref, bias_ref, oT_ref,
                 kmean_ref, sel_ref, acc_ref, m_ref, l_ref, *, nkb):
    h = pl.program_id(1)
    own = pl.program_id(2)
    par = h % 2
    qT2 = _pair_queries(qT_ref[0, 0], par)
    row = lax.broadcasted_iota(jnp.int32, (BLK, BLK), 0)
    col = lax.broadcasted_iota(jnp.int32, (BLK, BLK), 1)

    @pl.when(own == 0)
    def _():
        for n in range(nkb):
            kb = k_ref[0, n * BLK:(n + 1) * BLK, :].astype(F32)
            kmean_ref[n:n + 1, :] = jnp.mean(kb, axis=0, keepdims=True)

    gate = jnp.dot(kmean_ref[...], qT2.astype(F32), preferred_element_type=F32,
                   precision=lax.Precision.HIGHEST) * (HEAD_DIM ** 0.5)
    blk = lax.broadcasted_iota(jnp.int32, gate.shape, 0)
    g = jnp.where(blk < own, gate, NEG_INF)
    sel = jnp.zeros(gate.shape, F32)
    n_ok = jnp.minimum(own, MOBA_TOPK)
    for j in range(min(MOBA_TOPK, nkb)):
        top = jnp.max(g, axis=0, keepdims=True)
        idx = jnp.min(jnp.where(g == top, blk, nkb), axis=0, keepdims=True)
        pick = blk == idx
        sel = jnp.where(pick & (j < n_ok), 1.0, sel)
        g = jnp.where(pick, -jnp.inf, g)
    sel_ref[...] = sel

    def scores(n):
        start = pl.multiple_of(n * BLK, BLK)
        return jnp.dot(k_ref[0, pl.ds(start, BLK), :], qT2, preferred_element_type=F32)

    s = jnp.where(row <= col, scores(own) + bias_ref[0, 0], NEG_INF)
    m0 = jnp.max(s, axis=0, keepdims=True)
    p = jnp.exp(s - m0)
    m_ref[...] = m0
    l_ref[...] = jnp.sum(p, axis=0, keepdims=True)
    acc_ref[...] = jnp.dot(vT_ref[0, own], p.astype(BF16), preferred_element_type=F32)

    def past_block(n, bias):
        chosen = sel_ref[pl.ds(n, 1), :] > 0.0
        s = jnp.where(chosen, scores(n) + bias, NEG_INF)
        m_old = m_ref[...]
        m_new = jnp.maximum(m_old, jnp.max(s, axis=0, keepdims=True))
        alpha = jnp.exp(m_old - m_new)
        p = jnp.exp(s - m_new)
        m_ref[...] = m_new
        l_ref[...] = alpha * l_ref[...] + jnp.sum(p, axis=0, keepdims=True)
        acc_ref[...] = alpha * acc_ref[...] + jnp.dot(vT_ref[0, n], p.astype(BF16),
                                                      preferred_element_type=F32)

    @pl.when(own >= 1)
    def _():
        past_block(own - 1, bias_ref[0, 1])

    far_bias = far_ref[h]

    def body(n, carry):
        past_block(n, far_bias)
        return carry

    lax.fori_loop(0, jnp.maximum(own - 1, 0), body, 0)
    oT_ref[0] = acc_ref[...] / l_ref[...]


def _moba(qT, k, vT, bias, far):
    B, nkb, width, _ = qT.shape
    S = nkb * BLK
    q_spec, k_spec, v_spec, o_spec = _attn_specs(width)
    wrap = lambda spec: pl.BlockSpec(spec.block_shape, lambda b, h, q, far: spec.index_map(b, h, q))
    kern = functools.partial(_moba_kernel, nkb=nkb)
    return pl.pallas_call(
        kern,
        grid_spec=pltpu.PrefetchScalarGridSpec(
            num_scalar_prefetch=1,
            grid=(B, N_HEADS, nkb),
            in_specs=[wrap(q_spec), wrap(k_spec(S)), wrap(v_spec(nkb)),
                      pl.BlockSpec((1, 2, BLK, BLK), lambda b, h, q, far: (h, 0, 0, 0))],
            out_specs=wrap(o_spec),
            scratch_shapes=[pltpu.VMEM((nkb, 2 * HEAD_DIM), F32),
                            pltpu.VMEM((nkb, BLK), F32),
                            pltpu.VMEM((HEAD_DIM, BLK), F32),
                            pltpu.VMEM((1, BLK), F32),
                            pltpu.VMEM((1, BLK), F32)],
        ),
        out_shape=jax.ShapeDtypeStruct((B, width, S), F32),
        compiler_params=pltpu.CompilerParams(
            dimension_semantics=("arbitrary", "arbitrary", "arbitrary"),
            vmem_limit_bytes=VMEM_LIMIT_BYTES),
        name="moba",
    )(far, qT, k, vT, bias)


def _out_kernel(oT_ref, gT_ref, x_ref, woT_ref, lng_ref, lnb_ref, h_ref, yT_ref):
    g = gT_ref[0]
    og = (oT_ref[0] * (g * (1.0 / (1.0 + jnp.exp(-g))))).astype(BF16)
    yT_ref[...] = jnp.dot(woT_ref[...], og, preferred_element_type=F32)
    r = DEEPNORM_ALPHA * x_ref[0] + yT_ref[...].T
    mu = jnp.mean(r, axis=-1, keepdims=True)
    d = r - mu
    var = jnp.mean(d * d, axis=-1, keepdims=True)
    h_ref[0] = d * lax.rsqrt(var + LN_EPS) * lng_ref[...] + lnb_ref[...]


def _out_proj_ln(oT, gT, x, woT, lng, lnb):
    B, S, D = x.shape
    width = oT.shape[1]
    rows = min(PROJ_ROWS, S)
    return pl.pallas_call(
        _out_kernel,
        grid=(B, S // rows),
        in_specs=[
            pl.BlockSpec((1, width, rows), lambda b, s: (b, 0, s)),
            pl.BlockSpec((1, width, rows), lambda b, s: (b, 0, s)),
            pl.BlockSpec((1, rows, D), lambda b, s: (b, s, 0)),
            pl.BlockSpec((D, width), lambda b, s: (0, 0)),
            pl.BlockSpec((1, D), lambda b, s: (0, 0)),
            pl.BlockSpec((1, D), lambda b, s: (0, 0)),
        ],
        out_specs=pl.BlockSpec((1, rows, D), lambda b, s: (b, s, 0)),
        out_shape=jax.ShapeDtypeStruct((B, S, D), F32),
        scratch_shapes=[pltpu.VMEM((D, rows), F32)],
        compiler_params=pltpu.CompilerParams(
            dimension_semantics=("arbitrary", "arbitrary"), vmem_limit_bytes=VMEM_LIMIT_BYTES),
        name="outproj_ln",
    )(oT, gT, x, woT, lng, lnb)


def _suffix_sum_matrix():
    u = np.triu(np.ones((BLK, BLK), np.float32))
    return jnp.asarray(np.concatenate([u, u], axis=1), BF16)


def kernel(x, w_in, w_out, ln_g, ln_b, rel_table):
    width = N_HEADS * HEAD_DIM
    scale = HEAD_DIM ** -0.5
    u2 = _suffix_sum_matrix()
    bias = _bias_tiles(rel_table)
    far = rel_table[REL_BUCKETS - 1]
    h = x
    for layer in range(DEPTH):
        w = w_in[layer]
        wk = w[:, width:2 * width].astype(BF16)
        wt = jnp.concatenate([w[:, :width], w[:, 2 * width:]], axis=1).T.astype(BF16)
        woT = w_out[layer].T.astype(BF16)
        k, qT, vT, gT = _project(h, wk, wt, width=width, scale=scale)
        if layer % 2 == 0:
            oT = _stick_breaking(qT, k, vT, u2)
        else:
            oT = _moba(qT, k, vT, bias, far)
        h = _out_proj_ln(oT, gT, h, woT, ln_g[layer][None, :], ln_b[layer][None, :])
    return h
```

```python
import functools
import math

import jax
import jax.numpy as jnp
import numpy as np
from jax import lax
from jax.experimental import pallas as pl
from jax.experimental.pallas import tpu as pltpu

N_HEADS = 16
HEAD_DIM = 64
DEPTH = 2
MOBA_BLOCK = 256
MOBA_TOPK = 3
REL_BUCKETS = 32
REL_MAX_DIST = 128
LN_EPS = 1e-5
DEEPNORM_ALPHA = (2.0 * DEPTH) ** 0.25
NEG_INF = -1e30
LOG2E = math.log2(math.e)

BLK = 256
PROJ_ROWS = 512
ATTN_BATCH = 4
VMEM_LIMIT_BYTES = 56 * 1024 * 1024

F32 = jnp.float32
BF16 = jnp.bfloat16


def _proj_kernel(x_ref, wk_ref, wt_ref, k_ref, qT_ref, vT_ref, gT_ref, *, width, scale):
    xb = x_ref[0].astype(BF16)
    k_ref[0] = jnp.dot(xb, wk_ref[...], preferred_element_type=F32).astype(BF16)
    nt = (((1,), (1,)), ((), ()))
    n_blk = xb.shape[0] // BLK

    def t_part(i):
        return lax.dot_general(wt_ref[i * width:(i + 1) * width, :], xb, nt,
                               preferred_element_type=F32)

    qT = (t_part(0) * scale).astype(BF16)
    vT = t_part(1).astype(BF16)
    for j in range(n_blk):
        qT_ref[0, j] = qT[:, j * BLK:(j + 1) * BLK]
        vT_ref[0, j] = vT[:, j * BLK:(j + 1) * BLK]
    gT_ref[0] = t_part(2)


def _project(x, wk, wt, *, width, scale):
    B, S, D = x.shape
    rows = min(PROJ_ROWS, S)
    n_blk = rows // BLK
    nkb = S // BLK
    kern = functools.partial(_proj_kernel, width=width, scale=scale)
    return pl.pallas_call(
        kern,
        grid=(B, S // rows),
        in_specs=[
            pl.BlockSpec((1, rows, D), lambda b, s: (b, s, 0)),
            pl.BlockSpec((D, width), lambda b, s: (0, 0)),
            pl.BlockSpec((3 * width, D), lambda b, s: (0, 0)),
        ],
        out_specs=[
            pl.BlockSpec((1, rows, width), lambda b, s: (b, s, 0)),
            pl.BlockSpec((1, n_blk, width, BLK), lambda b, s: (b, s, 0, 0)),
            pl.BlockSpec((1, n_blk, width, BLK), lambda b, s: (b, s, 0, 0)),
            pl.BlockSpec((1, width, rows), lambda b, s: (b, 0, s)),
        ],
        out_shape=[
            jax.ShapeDtypeStruct((B, S, width), BF16),
            jax.ShapeDtypeStruct((B, nkb, width, BLK), BF16),
            jax.ShapeDtypeStruct((B, nkb, width, BLK), BF16),
            jax.ShapeDtypeStruct((B, width, S), F32),
        ],
        compiler_params=pltpu.CompilerParams(
            dimension_semantics=("arbitrary", "arbitrary"), vmem_limit_bytes=VMEM_LIMIT_BYTES),
        name="proj",
    )(x, wk, wt)


PAIR = 2 * HEAD_DIM


def _chains(nb):
    return [(b, hh) for b in range(nb) for hh in range(2)]


def _stage_pair_queries(qT_ref, q2_ref, nb):
    zeros = jnp.zeros((HEAD_DIM, BLK), BF16)
    for c, (b, hh) in enumerate(_chains(nb)):
        q = qT_ref[b, 0, hh * HEAD_DIM:(hh + 1) * HEAD_DIM, :]
        q2_ref[c] = jnp.concatenate([q, zeros] if hh == 0 else [zeros, q], axis=0)


def _attn_specs(nb, S, nkb, extra_args=0):
    def im(f):
        if extra_args:
            return lambda p, q, *_: f(p, q)
        return f
    q_spec = pl.BlockSpec((nb, 1, PAIR, BLK), im(lambda p, q: (0, q, p, 0)))
    k_spec = pl.BlockSpec((nb, S, PAIR), im(lambda p, q: (0, 0, p)))
    v_spec = pl.BlockSpec((nb, nkb, PAIR, BLK), im(lambda p, q: (0, 0, p, 0)))
    o_spec = pl.BlockSpec((nb, PAIR, BLK), im(lambda p, q: (0, p, q)))
    return q_spec, k_spec, v_spec, o_spec


def _sb_kernel(qT_ref, k_ref, vT_ref, u2_ref, oT_ref, q2_ref, acc_ref, r_ref, *, nb):
    qi = pl.program_id(1)
    chains = _chains(nb)
    _stage_pair_queries(qT_ref, q2_ref, nb)
    acc_ref[...] = jnp.zeros_like(acc_ref)
    r_ref[...] = jnp.zeros_like(r_ref)

    def block(kb, masked):
        start = pl.multiple_of(kb * BLK, BLK)
        if masked:
            row = lax.broadcasted_iota(jnp.int32, (BLK, BLK), 0)
            col = lax.broadcasted_iota(jnp.int32, (BLK, BLK), 1)
            past = row < col
        zs = [jnp.dot(k_ref[b, pl.ds(start, BLK), :], q2_ref[c], preferred_element_type=F32)
              for c, (b, hh) in enumerate(chains)]
        hls = []
        for z in zs:
            neg_abs = lax.bitcast_convert_type(
                lax.bitcast_convert_type(z, jnp.uint32) | jnp.uint32(0x80000000), F32)
            p = jnp.maximum(z, 0.0) + jnp.log(1.0 + jnp.exp2(neg_abs)) * LOG2E
            if masked:
                p = jnp.where(past, p, 0.0)
            hi = p.astype(BF16)
            lo = (p - hi.astype(F32)).astype(BF16)
            hls.append(jnp.concatenate([hi, lo], axis=0))
        css = [jnp.dot(u2_ref[...], hl, preferred_element_type=F32) + r_ref[c]
               for c, hl in enumerate(hls)]
        ws = []
        for c, (z, cs) in enumerate(zip(zs, css)):
            w = jnp.exp2(z - cs)
            if masked:
                w = jnp.where(past, w, 0.0)
            r_ref[c] = cs[0:1, :]
            ws.append(w.astype(BF16))
        for c, (b, hh) in enumerate(chains):
            acc_ref[c] += jnp.dot(vT_ref[b, kb, hh * HEAD_DIM:(hh + 1) * HEAD_DIM, :], ws[c],
                                  preferred_element_type=F32)

    block(qi, True)

    def body(i, carry):
        block(qi - 1 - i, False)
        return carry

    lax.fori_loop(0, qi, body, 0)
    for c, (b, hh) in enumerate(chains):
        oT_ref[b, hh * HEAD_DIM:(hh + 1) * HEAD_DIM, :] = acc_ref[c]


def _stick_breaking(qT, k, vT, u2):
    B, nkb, width, _ = qT.shape
    S = nkb * BLK
    nb = min(ATTN_BATCH, B)
    assert B == nb, "attention kernels take the whole batch in one grid step"
    q_spec, k_spec, v_spec, o_spec = _attn_specs(nb, S, nkb)
    nc = 2 * nb
    return pl.pallas_call(
        functools.partial(_sb_kernel, nb=nb),
        grid=(N_HEADS // 2, nkb),
        in_specs=[q_spec, k_spec, v_spec, pl.BlockSpec((BLK, 2 * BLK), lambda p, q: (0, 0))],
        out_specs=o_spec,
        out_shape=jax.ShapeDtypeStruct((B, width, S), F32),
        scratch_shapes=[pltpu.VMEM((nc, PAIR, BLK), BF16),
                        pltpu.VMEM((nc, HEAD_DIM, BLK), F32),
                        pltpu.VMEM((nc, 1, BLK), F32)],
        compiler_params=pltpu.CompilerParams(
            dimension_semantics=("arbitrary", "arbitrary"), vmem_limit_bytes=VMEM_LIMIT_BYTES),
        name="stickbreak",
    )(qT, k, vT, u2)


def _bias_kernel(table_ref, bucket_ref, bias_ref):
    h = pl.program_id(0)
    for t in range(2):
        bk = bucket_ref[t]
        out = jnp.full(bk.shape, NEG_INF, F32)
        for i in range(REL_BUCKETS):
            out = jnp.where(bk == i, table_ref[i * N_HEADS + h] * LOG2E, out)
        bias_ref[0, t] = out


def _t5_bucket_np(dist):
    n = np.maximum(dist, 0)
    max_exact = REL_BUCKETS // 2
    nf = np.maximum(n, 1).astype(np.float64)
    large = max_exact + (np.log(nf / max_exact) / math.log(REL_MAX_DIST / max_exact)
                         * (REL_BUCKETS - max_exact)).astype(np.int32)
    large = np.minimum(large, REL_BUCKETS - 1)
    return np.where(n < max_exact, n, large).astype(np.int32)


def _bias_tiles(rel_table):
    s = np.arange(BLK)[:, None]
    t = np.arange(BLK)[None, :]
    own = np.where(t - s >= 0, _t5_bucket_np(t - s), -1)
    buckets = np.stack([own, _t5_bucket_np(t - s + BLK)]).astype(np.int32)
    return pl.pallas_call(
        _bias_kernel,
        grid_spec=pltpu.PrefetchScalarGridSpec(
            num_scalar_prefetch=1,
            grid=(N_HEADS,),
            in_specs=[pl.BlockSpec((2, BLK, BLK), lambda h, tab: (0, 0, 0))],
            out_specs=pl.BlockSpec((1, 2, BLK, BLK), lambda h, tab: (h, 0, 0, 0)),
        ),
        out_shape=jax.ShapeDtypeStruct((N_HEADS, 2, BLK, BLK), F32),
        name="moba_bias",
    )(rel_table.reshape(-1), jnp.asarray(buckets))


def _moba_kernel(far_ref, qT_ref, k_ref, vT_ref, bias_ref, oT_ref,
                 q2_ref, kmean_ref, sel_ref, acc_ref, m_ref, l_ref, *, nb, nkb, gate_scale):
    pair = pl.program_id(0)
    own = pl.program_id(1)
    chains = _chains(nb)
    _stage_pair_queries(qT_ref, q2_ref, nb)

    @pl.when(own == 0)
    def _():
        for b in range(nb):
            for n in range(nkb):
                kb = k_ref[b, n * BLK:(n + 1) * BLK, :].astype(F32)
                kmean_ref[b, n:n + 1, :] = jnp.mean(kb, axis=0, keepdims=True)

    n_ok = jnp.minimum(own, MOBA_TOPK)
    for c, (b, hh) in enumerate(chains):
        gate = jnp.dot(kmean_ref[b], q2_ref[c].astype(F32), preferred_element_type=F32,
                       precision=lax.Precision.HIGHEST) * gate_scale
        blk = lax.broadcasted_iota(jnp.int32, gate.shape, 0)
        g = jnp.where(blk < own, gate, NEG_INF)
        sel = jnp.zeros(gate.shape, F32)
        for j in range(min(MOBA_TOPK, nkb)):
            top = jnp.max(g, axis=0, keepdims=True)
            idx = jnp.min(jnp.where(g == top, blk, nkb), axis=0, keepdims=True)
            pick = blk == idx
            sel = jnp.where(pick & (j < n_ok), 1.0, sel)
            g = jnp.where(pick, -jnp.inf, g)
        sel_ref[c] = sel

    def scores(b, c, n):
        start = pl.multiple_of(n * BLK, BLK)
        return jnp.dot(k_ref[b, pl.ds(start, BLK), :], q2_ref[c], preferred_element_type=F32)

    def v_blk(b, hh, n):
        return vT_ref[b, n, hh * HEAD_DIM:(hh + 1) * HEAD_DIM, :]

    zs = [scores(b, c, own) for c, (b, hh) in enumerate(chains)]
    ps = []
    for c, (b, hh) in enumerate(chains):
        s = zs[c] + bias_ref[hh, 0]
        m0 = jnp.max(s, axis=0, keepdims=True)
        p = jnp.exp2(s - m0)
        m_ref[c] = m0
        l_ref[c] = jnp.sum(p, axis=0, keepdims=True)
        ps.append(p.astype(BF16))
    for c, (b, hh) in enumerate(chains):
        acc_ref[c] = jnp.dot(v_blk(b, hh, own), ps[c], preferred_element_type=F32)

    def past_block(n, bias_of):
        zs = [scores(b, c, n) for c, (b, hh) in enumerate(chains)]
        ps, alphas = [], []
        for c, (b, hh) in enumerate(chains):
            chosen = sel_ref[c, pl.ds(n, 1), :] > 0.0
            s, shift = bias_of(hh, zs[c])
            m_old = m_ref[c]
            m_blk = jnp.where(chosen, jnp.max(s, axis=0, keepdims=True) + shift, NEG_INF)
            m_new = jnp.maximum(m_old, m_blk)
            alpha = jnp.exp2(m_old - m_new)
            p = jnp.exp2(s - jnp.where(chosen, m_new - shift, -NEG_INF))
            m_ref[c] = m_new
            l_ref[c] = alpha * l_ref[c] + jnp.sum(p, axis=0, keepdims=True)
            ps.append(p.astype(BF16))
            alphas.append(alpha)
        for c, (b, hh) in enumerate(chains):
            acc_ref[c] = alphas[c] * acc_ref[c] + jnp.dot(v_blk(b, hh, n), ps[c],
                                                          preferred_element_type=F32)

    @pl.when(own >= 1)
    def _():
        past_block(own - 1, lambda hh, z: (z + bias_ref[hh, 1], 0.0))

    far = [far_ref[2 * pair + hh] * LOG2E for hh in range(2)]

    def body(n, carry):
        past_block(n, lambda hh, z: (z, far[hh]))
        return carry

    lax.fori_loop(0, jnp.maximum(own - 1, 0), body, 0)
    for c, (b, hh) in enumerate(chains):
        oT_ref[b, hh * HEAD_DIM:(hh + 1) * HEAD_DIM, :] = acc_ref[c] / l_ref[c]


def _moba(qT, k, vT, bias, far, *, gate_scale):
    B, nkb, width, _ = qT.shape
    S = nkb * BLK
    nb = min(ATTN_BATCH, B)
    assert B == nb, "attention kernels take the whole batch in one grid step"
    q_spec, k_spec, v_spec, o_spec = _attn_specs(nb, S, nkb, extra_args=1)
    nc = 2 * nb
    kern = functools.partial(_moba_kernel, nb=nb, nkb=nkb, gate_scale=gate_scale)
    return pl.pallas_call(
        kern,
        grid_spec=pltpu.PrefetchScalarGridSpec(
            num_scalar_prefetch=1,
            grid=(N_HEADS // 2, nkb),
            in_specs=[q_spec, k_spec, v_spec,
                      pl.BlockSpec((2, 2, BLK, BLK), lambda p, q, far: (p, 0, 0, 0))],
            out_specs=o_spec,
            scratch_shapes=[pltpu.VMEM((nc, PAIR, BLK), BF16),
                            pltpu.VMEM((nb, nkb, PAIR), F32),
                            pltpu.VMEM((nc, nkb, BLK), F32),
                            pltpu.VMEM((nc, HEAD_DIM, BLK), F32),
                            pltpu.VMEM((nc, 1, BLK), F32),
                            pltpu.VMEM((nc, 1, BLK), F32)],
        ),
        out_shape=jax.ShapeDtypeStruct((B, width, S), F32),
        compiler_params=pltpu.CompilerParams(
            dimension_semantics=("arbitrary", "arbitrary"), vmem_limit_bytes=VMEM_LIMIT_BYTES),
        name="moba",
    )(far, qT, k, vT, bias)


def _out_kernel(oT_ref, gT_ref, x_ref, woT_ref, lng_ref, lnb_ref, h_ref, yT_ref):
    g = gT_ref[0]
    og = (oT_ref[0] * (g * (1.0 / (1.0 + jnp.exp(-g))))).astype(BF16)
    yT_ref[...] = jnp.dot(woT_ref[...], og, preferred_element_type=F32)
    r = DEEPNORM_ALPHA * x_ref[0] + yT_ref[...].T
    mu = jnp.mean(r, axis=-1, keepdims=True)
    d = r - mu
    var = jnp.mean(d * d, axis=-1, keepdims=True)
    h_ref[0] = d * lax.rsqrt(var + LN_EPS) * lng_ref[...] + lnb_ref[...]


def _out_proj_ln(oT, gT, x, woT, lng, lnb):
    B, S, D = x.shape
    width = oT.shape[1]
    rows = min(PROJ_ROWS, S)
    return pl.pallas_call(
        _out_kernel,
        grid=(B, S // rows),
        in_specs=[
            pl.BlockSpec((1, width, rows), lambda b, s: (b, 0, s)),
            pl.BlockSpec((1, width, rows), lambda b, s: (b, 0, s)),
            pl.BlockSpec((1, rows, D), lambda b, s: (b, s, 0)),
            pl.BlockSpec((D, width), lambda b, s: (0, 0)),
            pl.BlockSpec((1, D), lambda b, s: (0, 0)),
            pl.BlockSpec((1, D), lambda b, s: (0, 0)),
        ],
        out_specs=pl.BlockSpec((1, rows, D), lambda b, s: (b, s, 0)),
        out_shape=jax.ShapeDtypeStruct((B, S, D), F32),
        scratch_shapes=[pltpu.VMEM((D, rows), F32)],
        compiler_params=pltpu.CompilerParams(
            dimension_semantics=("arbitrary", "arbitrary"), vmem_limit_bytes=VMEM_LIMIT_BYTES),
        name="outproj_ln",
    )(oT, gT, x, woT, lng, lnb)


def _suffix_sum_matrix():
    u = np.triu(np.ones((BLK, BLK), np.float32))
    return jnp.asarray(np.concatenate([u, u], axis=1), BF16)


def kernel(x, w_in, w_out, ln_g, ln_b, rel_table):
    width = N_HEADS * HEAD_DIM
    scale = HEAD_DIM ** -0.5 * LOG2E
    u2 = _suffix_sum_matrix()
    bias = _bias_tiles(rel_table)
    far = rel_table[REL_BUCKETS - 1]
    h = x
    for layer in range(DEPTH):
        w = w_in[layer]
        wk = w[:, width:2 * width].astype(BF16)
        wt = jnp.concatenate([w[:, :width], w[:, 2 * width:]], axis=1).T.astype(BF16)
        woT = w_out[layer].T.astype(BF16)
        k, qT, vT, gT = _project(h, wk, wt, width=width, scale=scale)
        if layer % 2 == 0:
            oT = _stick_breaking(qT, k, vT, u2)
        else:
            oT = _moba(qT, k, vT, bias, far, gate_scale=1.0 / scale)
        h = _out_proj_ln(oT, gT, h, woT, ln_g[layer][None, :], ln_b[layer][None, :])
    return h
```

```python
import functools
import math

import jax
import jax.numpy as jnp
import numpy as np
from jax import lax
from jax.experimental import pallas as pl
from jax.experimental.pallas import tpu as pltpu

N_HEADS = 16
HEAD_DIM = 64
DEPTH = 2
MOBA_BLOCK = 256
MOBA_TOPK = 3
REL_BUCKETS = 32
REL_MAX_DIST = 128
LN_EPS = 1e-5
DEEPNORM_ALPHA = (2.0 * DEPTH) ** 0.25
NEG_INF = -1e30
LOG2E = math.log2(math.e)

BLK = 256
PROJ_ROWS = 512
ATTN_BATCH = 4
SB_GROUP = 2
SB_LAG = (2, 4)
MOBA_GROUP = 2
VMEM_LIMIT_BYTES = 56 * 1024 * 1024

F32 = jnp.float32
BF16 = jnp.bfloat16


def _proj_kernel(x_ref, wk_ref, wt_ref, k_ref, qT_ref, vT_ref, gT_ref, *, width, scale):
    xb = x_ref[0].astype(BF16)
    k_ref[0] = jnp.dot(xb, wk_ref[...], preferred_element_type=F32).astype(BF16)
    nt = (((1,), (1,)), ((), ()))
    n_blk = xb.shape[0] // BLK

    def t_part(i):
        return lax.dot_general(wt_ref[i * width:(i + 1) * width, :], xb, nt,
                               preferred_element_type=F32)

    qT = (t_part(0) * scale).astype(BF16)
    vT = t_part(1).astype(BF16)
    for j in range(n_blk):
        qT_ref[0, j] = qT[:, j * BLK:(j + 1) * BLK]
        vT_ref[0, j] = vT[:, j * BLK:(j + 1) * BLK]
    gT_ref[0] = t_part(2)


def _project(x, wk, wt, *, width, scale):
    B, S, D = x.shape
    rows = min(PROJ_ROWS, S)
    n_blk = rows // BLK
    nkb = S // BLK
    kern = functools.partial(_proj_kernel, width=width, scale=scale)
    return pl.pallas_call(
        kern,
        grid=(B, S // rows),
        in_specs=[
            pl.BlockSpec((1, rows, D), lambda b, s: (b, s, 0)),
            pl.BlockSpec((D, width), lambda b, s: (0, 0)),
            pl.BlockSpec((3 * width, D), lambda b, s: (0, 0)),
        ],
        out_specs=[
            pl.BlockSpec((1, rows, width), lambda b, s: (b, s, 0)),
            pl.BlockSpec((1, n_blk, width, BLK), lambda b, s: (b, s, 0, 0)),
            pl.BlockSpec((1, n_blk, width, BLK), lambda b, s: (b, s, 0, 0)),
            pl.BlockSpec((1, width, rows), lambda b, s: (b, 0, s)),
        ],
        out_shape=[
            jax.ShapeDtypeStruct((B, S, width), BF16),
            jax.ShapeDtypeStruct((B, nkb, width, BLK), BF16),
            jax.ShapeDtypeStruct((B, nkb, width, BLK), BF16),
            jax.ShapeDtypeStruct((B, width, S), F32),
        ],
        compiler_params=pltpu.CompilerParams(
            dimension_semantics=("arbitrary", "arbitrary"), vmem_limit_bytes=VMEM_LIMIT_BYTES),
        name="proj",
    )(x, wk, wt)


PAIR = 2 * HEAD_DIM


def _chains(nb):
    return [(b, hh) for b in range(nb) for hh in range(2)]


def _stage_pair_queries(qT_ref, q2_ref, nb):
    zeros = jnp.zeros((HEAD_DIM, BLK), BF16)
    for c, (b, hh) in enumerate(_chains(nb)):
        q = qT_ref[b, 0, hh * HEAD_DIM:(hh + 1) * HEAD_DIM, :]
        q2_ref[c] = jnp.concatenate([q, zeros] if hh == 0 else [zeros, q], axis=0)


def _attn_specs(nb, S, nkb, extra_args=0):
    def im(f):
        if extra_args:
            return lambda g, p, q, *_: f(g, p, q)
        return f
    q_spec = pl.BlockSpec((nb, 1, PAIR, BLK), im(lambda g, p, q: (g, q, p, 0)))
    k_spec = pl.BlockSpec((nb, S, PAIR), im(lambda g, p, q: (g, 0, p)))
    v_spec = pl.BlockSpec((nb, nkb, PAIR, BLK), im(lambda g, p, q: (g, 0, p, 0)))
    o_spec = pl.BlockSpec((nb, PAIR, BLK), im(lambda g, p, q: (g, p, q)))
    return q_spec, k_spec, v_spec, o_spec


def _sb_kernel(qT_ref, k_ref, vT_ref, us_ref, oT_ref, q2_ref, acc_ref, r_ref, *, nb):
    qi = pl.program_id(2)
    chains = _chains(nb)
    _stage_pair_queries(qT_ref, q2_ref, nb)
    acc_ref[...] = jnp.zeros_like(acc_ref)
    r_ref[...] = jnp.zeros_like(r_ref)

    def process(kbs, masked):
        if masked:
            row = lax.broadcasted_iota(jnp.int32, (BLK, BLK), 0)
            col = lax.broadcasted_iota(jnp.int32, (BLK, BLK), 1)
            past = row < col
        tiles = [(kb, c, b, hh) for kb in kbs for c, (b, hh) in enumerate(chains)]
        n_tiles = len(tiles)
        zs, ys, ps, css, pvs = {}, {}, {}, {}, {}
        lag_p, lag_w = SB_LAG
        for step in range(n_tiles + lag_w):
            if step < n_tiles:
                kb, c, b, hh = tiles[step]
                zs[step] = jnp.dot(k_ref[b, pl.ds(pl.multiple_of(kb * BLK, BLK), BLK), :], q2_ref[c],
                                   preferred_element_type=F32)
            i = step - lag_p
            if 0 <= i < n_tiles:
                z = zs.pop(i)
                neg_abs = lax.bitcast_convert_type(
                    lax.bitcast_convert_type(z, jnp.uint32) | jnp.uint32(0x80000000), F32)
                p = jnp.maximum(z, 0.0) + jnp.log(1.0 + jnp.exp2(neg_abs)) * LOG2E
                if masked:
                    p = jnp.where(past, p, 0.0)
                ys[i] = z - p
                ps[i] = p.astype(BF16)
                css[i] = jnp.dot(us_ref[...], ps[i], preferred_element_type=F32)
            j = step - lag_w
            if 0 <= j < n_tiles:
                kb, c, b, hh = tiles[j]
                w = jnp.exp2(ys.pop(j) - css[j])
                if masked:
                    w = jnp.where(past, w, 0.0)
                pvs[j] = jnp.dot(vT_ref[b, kb, hh * HEAD_DIM:(hh + 1) * HEAD_DIM, :], w.astype(BF16),
                                 preferred_element_type=F32)
        for c in range(len(chains)):
            r = r_ref[c]
            acc = acc_ref[c]
            for t, (kb, tc, b, hh) in enumerate(tiles):
                if tc == c:
                    acc = acc + pvs[t] * jnp.exp2(-r)
                    r = r + css[t][0:1, :] + ps[t][0:1, :].astype(F32)
            acc_ref[c] = acc
            r_ref[c] = r

    process([qi], True)
    rem = qi % SB_GROUP

    def single(i, carry):
        process([qi - 1 - i], False)
        return carry

    def group(j, carry):
        first = qi - 1 - rem - SB_GROUP * j
        process([first - u for u in range(SB_GROUP)], False)
        return carry

    lax.fori_loop(0, rem, single, 0)
    lax.fori_loop(0, qi // SB_GROUP, group, 0)
    for c, (b, hh) in enumerate(chains):
        oT_ref[b, hh * HEAD_DIM:(hh + 1) * HEAD_DIM, :] = acc_ref[c]


def _stick_breaking(qT, k, vT, us):
    B, nkb, width, _ = qT.shape
    S = nkb * BLK
    nb = min(ATTN_BATCH, B)
    assert B % nb == 0
    q_spec, k_spec, v_spec, o_spec = _attn_specs(nb, S, nkb)
    nc = 2 * nb
    return pl.pallas_call(
        functools.partial(_sb_kernel, nb=nb),
        grid=(B // nb, N_HEADS // 2, nkb),
        in_specs=[q_spec, k_spec, v_spec, pl.BlockSpec((BLK, BLK), lambda g, p, q: (0, 0))],
        out_specs=o_spec,
        out_shape=jax.ShapeDtypeStruct((B, width, S), F32),
        scratch_shapes=[pltpu.VMEM((nc, PAIR, BLK), BF16),
                        pltpu.VMEM((nc, HEAD_DIM, BLK), F32),
                        pltpu.VMEM((nc, 1, BLK), F32)],
        compiler_params=pltpu.CompilerParams(
            dimension_semantics=("arbitrary", "arbitrary", "arbitrary"),
            vmem_limit_bytes=VMEM_LIMIT_BYTES),
        name="stickbreak",
    )(qT, k, vT, us)


def _bias_kernel(table_ref, bucket_ref, bias_ref):
    h = pl.program_id(0)
    for t in range(2):
        bk = bucket_ref[t]
        out = jnp.full(bk.shape, NEG_INF, F32)
        for i in range(REL_BUCKETS):
            out = jnp.where(bk == i, table_ref[i * N_HEADS + h] * LOG2E, out)
        bias_ref[0, t] = out


def _t5_bucket_np(dist):
    n = np.maximum(dist, 0)
    max_exact = REL_BUCKETS // 2
    nf = np.maximum(n, 1).astype(np.float64)
    large = max_exact + (np.log(nf / max_exact) / math.log(REL_MAX_DIST / max_exact)
                         * (REL_BUCKETS - max_exact)).astype(np.int32)
    large = np.minimum(large, REL_BUCKETS - 1)
    return np.where(n < max_exact, n, large).astype(np.int32)


def _bias_tiles(rel_table):
    s = np.arange(BLK)[:, None]
    t = np.arange(BLK)[None, :]
    own = np.where(t - s >= 0, _t5_bucket_np(t - s), -1)
    buckets = np.stack([own, _t5_bucket_np(t - s + BLK)]).astype(np.int32)
    return pl.pallas_call(
        _bias_kernel,
        grid_spec=pltpu.PrefetchScalarGridSpec(
            num_scalar_prefetch=1,
            grid=(N_HEADS,),
            in_specs=[pl.BlockSpec((2, BLK, BLK), lambda h, tab: (0, 0, 0))],
            out_specs=pl.BlockSpec((1, 2, BLK, BLK), lambda h, tab: (h, 0, 0, 0)),
        ),
        out_shape=jax.ShapeDtypeStruct((N_HEADS, 2, BLK, BLK), F32),
        name="moba_bias",
    )(rel_table.reshape(-1), jnp.asarray(buckets))


def _moba_kernel(far_ref, qT_ref, k_ref, vT_ref, bias_ref, oT_ref,
                 q2_ref, kmean_ref, sel_ref, acc_ref, m_ref, l_ref, *, nb, nkb, gate_scale):
    pair = pl.program_id(1)
    own = pl.program_id(2)
    chains = _chains(nb)
    _stage_pair_queries(qT_ref, q2_ref, nb)

    @pl.when(own == 0)
    def _():
        for b in range(nb):
            for n in range(nkb):
                kb = k_ref[b, n * BLK:(n + 1) * BLK, :].astype(F32)
                km = jnp.mean(kb, axis=0, keepdims=True)
                for part in range(3):
                    piece = km.astype(BF16)
                    kmean_ref[b, part * nkb + n:part * nkb + n + 1, :] = piece
                    km = km - piece.astype(F32)

    def scores(n):
        start = pl.multiple_of(n * BLK, BLK)
        return [jnp.dot(k_ref[b, pl.ds(start, BLK), :], q2_ref[c], preferred_element_type=F32)
                for c, (b, hh) in enumerate(chains)]

    def v_blk(b, hh, n):
        return vT_ref[b, n, hh * HEAD_DIM:(hh + 1) * HEAD_DIM, :]

    prev = jnp.maximum(own - 1, 0)
    zs_own = scores(own)
    zs_prev = scores(prev)

    n_ok = jnp.minimum(own, MOBA_TOPK)
    for c, (b, hh) in enumerate(chains):
        parts = jnp.dot(kmean_ref[b], q2_ref[c], preferred_element_type=F32)
        gate = (parts[:nkb] + parts[nkb:2 * nkb] + parts[2 * nkb:]) * gate_scale
        blk = lax.broadcasted_iota(jnp.int32, gate.shape, 0)
        g = jnp.where(blk < own, gate, NEG_INF)
        sel = jnp.zeros(gate.shape, F32)
        for j in range(min(MOBA_TOPK, nkb)):
            top = jnp.max(g, axis=0, keepdims=True)
            idx = jnp.min(jnp.where(g == top, blk, nkb), axis=0, keepdims=True)
            pick = blk == idx
            sel = jnp.where(pick & (j < n_ok), 1.0, sel)
            g = jnp.where(pick, -jnp.inf, g)
        sel_ref[c] = sel

    ps = []
    for c, (b, hh) in enumerate(chains):
        s = zs_own[c] + bias_ref[hh, 0]
        m0 = jnp.max(s, axis=0, keepdims=True)
        p = jnp.exp2(s - m0)
        m_ref[c] = m0
        l_ref[c] = jnp.sum(p, axis=0, keepdims=True)
        ps.append(p.astype(BF16))
    for c, (b, hh) in enumerate(chains):
        acc_ref[c] = jnp.dot(v_blk(b, hh, own), ps[c], preferred_element_type=F32)

    def past_blocks(zss, ns, bias_of):
        pss, alphas = [], []
        for c, (b, hh) in enumerate(chains):
            m_old = m_ref[c]
            m_new = m_old
            terms = []
            for zs, n in zip(zss, ns):
                chosen = sel_ref[c, pl.ds(n, 1), :] > 0.0
                s, shift = bias_of(hh, zs[c])
                m_blk = jnp.where(chosen, jnp.max(s, axis=0, keepdims=True) + shift, NEG_INF)
                m_new = jnp.maximum(m_new, m_blk)
                terms.append((chosen, s, shift))
            alpha = jnp.exp2(m_old - m_new)
            l = alpha * l_ref[c]
            ps = []
            for chosen, s, shift in terms:
                p = jnp.exp2(s - jnp.where(chosen, m_new - shift, -NEG_INF))
                l = l + jnp.sum(p, axis=0, keepdims=True)
                ps.append(p.astype(BF16))
            m_ref[c] = m_new
            l_ref[c] = l
            pss.append(ps)
            alphas.append(alpha)
        for c, (b, hh) in enumerate(chains):
            acc = alphas[c] * acc_ref[c]
            for p, n in zip(pss[c], ns):
                acc = acc + jnp.dot(v_blk(b, hh, n), p, preferred_element_type=F32)
            acc_ref[c] = acc

    past_blocks([zs_prev], [prev], lambda hh, z: (z + bias_ref[hh, 1], 0.0))

    far = [far_ref[2 * pair + hh] * LOG2E for hh in range(2)]
    far_bias = lambda hh, z: (z, far[hh])
    n_far = jnp.maximum(own - 1, 0)
    rem = n_far % MOBA_GROUP

    def single(i, carry):
        past_blocks([scores(i)], [i], far_bias)
        return carry

    def group(j, carry):
        ns = [rem + MOBA_GROUP * j + u for u in range(MOBA_GROUP)]
        past_blocks([scores(n) for n in ns], ns, far_bias)
        return carry

    lax.fori_loop(0, rem, single, 0)
    lax.fori_loop(0, n_far // MOBA_GROUP, group, 0)
    for c, (b, hh) in enumerate(chains):
        oT_ref[b, hh * HEAD_DIM:(hh + 1) * HEAD_DIM, :] = acc_ref[c] / l_ref[c]


def _moba(qT, k, vT, bias, far, *, gate_scale):
    B, nkb, width, _ = qT.shape
    S = nkb * BLK
    nb = min(ATTN_BATCH, B)
    assert B % nb == 0
    q_spec, k_spec, v_spec, o_spec = _attn_specs(nb, S, nkb, extra_args=1)
    nc = 2 * nb
    kern = functools.partial(_moba_kernel, nb=nb, nkb=nkb, gate_scale=gate_scale)
    return pl.pallas_call(
        kern,
        grid_spec=pltpu.PrefetchScalarGridSpec(
            num_scalar_prefetch=1,
            grid=(B // nb, N_HEADS // 2, nkb),
            in_specs=[q_spec, k_spec, v_spec,
                      pl.BlockSpec((2, 2, BLK, BLK), lambda g, p, q, far: (p, 0, 0, 0))],
            out_specs=o_spec,
            scratch_shapes=[pltpu.VMEM((nc, PAIR, BLK), BF16),
                            pltpu.VMEM((nb, 3 * nkb, PAIR), BF16),
                            pltpu.VMEM((nc, nkb, BLK), F32),
                            pltpu.VMEM((nc, HEAD_DIM, BLK), F32),
                            pltpu.VMEM((nc, 1, BLK), F32),
                            pltpu.VMEM((nc, 1, BLK), F32)],
        ),
        out_shape=jax.ShapeDtypeStruct((B, width, S), F32),
        compiler_params=pltpu.CompilerParams(
            dimension_semantics=("arbitrary", "arbitrary", "arbitrary"),
            vmem_limit_bytes=VMEM_LIMIT_BYTES),
        name="moba",
    )(far, qT, k, vT, bias)


def _out_kernel(oT_ref, gT_ref, x_ref, woT_ref, lng_ref, lnb_ref, h_ref, yT_ref):
    g = gT_ref[0]
    og = (oT_ref[0] * (g * (1.0 / (1.0 + jnp.exp(-g))))).astype(BF16)
    yT_ref[...] = jnp.dot(woT_ref[...], og, preferred_element_type=F32)
    r = DEEPNORM_ALPHA * x_ref[0] + yT_ref[...].T
    mu = jnp.mean(r, axis=-1, keepdims=True)
    d = r - mu
    var = jnp.mean(d * d, axis=-1, keepdims=True)
    h_ref[0] = d * lax.rsqrt(var + LN_EPS) * lng_ref[...] + lnb_ref[...]


def _out_proj_ln(oT, gT, x, woT, lng, lnb):
    B, S, D = x.shape
    width = oT.shape[1]
    rows = min(PROJ_ROWS, S)
    return pl.pallas_call(
        _out_kernel,
        grid=(B, S // rows),
        in_specs=[
            pl.BlockSpec((1, width, rows), lambda b, s: (b, 0, s)),
            pl.BlockSpec((1, width, rows), lambda b, s: (b, 0, s)),
            pl.BlockSpec((1, rows, D), lambda b, s: (b, s, 0)),
            pl.BlockSpec((D, width), lambda b, s: (0, 0)),
            pl.BlockSpec((1, D), lambda b, s: (0, 0)),
            pl.BlockSpec((1, D), lambda b, s: (0, 0)),
        ],
        out_specs=pl.BlockSpec((1, rows, D), lambda b, s: (b, s, 0)),
        out_shape=jax.ShapeDtypeStruct((B, S, D), F32),
        scratch_shapes=[pltpu.VMEM((D, rows), F32)],
        compiler_params=pltpu.CompilerParams(
            dimension_semantics=("arbitrary", "arbitrary"), vmem_limit_bytes=VMEM_LIMIT_BYTES),
        name="outproj_ln",
    )(oT, gT, x, woT, lng, lnb)


def _suffix_sum_matrix():
    return jnp.asarray(np.triu(np.ones((BLK, BLK), np.float32), k=1), BF16)


def kernel(x, w_in, w_out, ln_g, ln_b, rel_table):
    width = N_HEADS * HEAD_DIM
    scale = HEAD_DIM ** -0.5 * LOG2E
    us = _suffix_sum_matrix()
    bias = _bias_tiles(rel_table)
    far = rel_table[REL_BUCKETS - 1]
    h = x
    for layer in range(DEPTH):
        w = w_in[layer]
        wk = w[:, width:2 * width].astype(BF16)
        wt = jnp.concatenate([w[:, :width], w[:, 2 * width:]], axis=1).T.astype(BF16)
        woT = w_out[layer].T.astype(BF16)
        k, qT, vT, gT = _project(h, wk, wt, width=width, scale=scale)
        if layer % 2 == 0:
            oT = _stick_breaking(qT, k, vT, us)
        else:
            oT = _moba(qT, k, vT, bias, far, gate_scale=1.0 / scale)
        h = _out_proj_ln(oT, gT, h, woT, ln_g[layer][None, :], ln_b[layer][None, :])
    return h
```

```python
import functools
import math

import jax
import jax.numpy as jnp
import numpy as np
from jax import lax
from jax.experimental import pallas as pl
from jax.experimental.pallas import tpu as pltpu

N_HEADS = 16
HEAD_DIM = 64
DEPTH = 2
MOBA_BLOCK = 256
MOBA_TOPK = 3
REL_BUCKETS = 32
REL_MAX_DIST = 128
LN_EPS = 1e-5
DEEPNORM_ALPHA = (2.0 * DEPTH) ** 0.25
NEG_INF = -1e30
LOG2E = math.log2(math.e)
EXP2_CLAMP = 100.0

BLK = 256
PROJ_ROWS = 512
ATTN_BATCH = 4
SB_GROUP = 3
SB_LAG = (2, 4)
MOBA_GROUP = 2
SUM_ROWS = 16
VMEM_LIMIT_BYTES = 56 * 1024 * 1024

F32 = jnp.float32
BF16 = jnp.bfloat16


def _proj_kernel(x_ref, wk_ref, wt_ref, k_ref, qT_ref, vT_ref, gT_ref, *, width, scale):
    xb = x_ref[0].astype(BF16)
    k_ref[0] = jnp.dot(xb, wk_ref[...], preferred_element_type=F32).astype(BF16)
    nt = (((1,), (1,)), ((), ()))
    n_blk = xb.shape[0] // BLK

    def t_part(i):
        return lax.dot_general(wt_ref[i * width:(i + 1) * width, :], xb, nt,
                               preferred_element_type=F32)

    qT = (t_part(0) * scale).astype(BF16)
    vT = t_part(1).astype(BF16)
    for j in range(n_blk):
        qT_ref[0, j] = qT[:, j * BLK:(j + 1) * BLK]
        vT_ref[0, j] = vT[:, j * BLK:(j + 1) * BLK]
    gT_ref[0] = t_part(2).astype(BF16)


def _project(x, wk, wt, *, width, scale):
    B, S, D = x.shape
    rows = min(PROJ_ROWS, S)
    n_blk = rows // BLK
    nkb = S // BLK
    kern = functools.partial(_proj_kernel, width=width, scale=scale)
    return pl.pallas_call(
        kern,
        grid=(B, S // rows),
        in_specs=[
            pl.BlockSpec((1, rows, D), lambda b, s: (b, s, 0)),
            pl.BlockSpec((D, width), lambda b, s: (0, 0)),
            pl.BlockSpec((3 * width, D), lambda b, s: (0, 0)),
        ],
        out_specs=[
            pl.BlockSpec((1, rows, width), lambda b, s: (b, s, 0)),
            pl.BlockSpec((1, n_blk, width, BLK), lambda b, s: (b, s, 0, 0)),
            pl.BlockSpec((1, n_blk, width, BLK), lambda b, s: (b, s, 0, 0)),
            pl.BlockSpec((1, width, rows), lambda b, s: (b, 0, s)),
        ],
        out_shape=[
            jax.ShapeDtypeStruct((B, S, width), BF16),
            jax.ShapeDtypeStruct((B, nkb, width, BLK), BF16),
            jax.ShapeDtypeStruct((B, nkb, width, BLK), BF16),
            jax.ShapeDtypeStruct((B, width, S), BF16),
        ],
        compiler_params=pltpu.CompilerParams(
            dimension_semantics=("arbitrary", "arbitrary"), vmem_limit_bytes=VMEM_LIMIT_BYTES),
        name="proj",
    )(x, wk, wt)


PAIR = 2 * HEAD_DIM


def _chains(nb):
    return [(b, hh) for b in range(nb) for hh in range(2)]


def _stage_pair_queries(qT_ref, q2_ref, nb):
    zeros = jnp.zeros((HEAD_DIM, BLK), BF16)
    for c, (b, hh) in enumerate(_chains(nb)):
        q = qT_ref[b, 0, hh * HEAD_DIM:(hh + 1) * HEAD_DIM, :]
        q2_ref[c] = jnp.concatenate([q, zeros] if hh == 0 else [zeros, q], axis=0)


def _attn_specs(nb, S, nkb, extra_args=0):
    def im(f):
        if extra_args:
            return lambda g, p, q, *_: f(g, p, q)
        return f
    q_spec = pl.BlockSpec((nb, 1, PAIR, BLK), im(lambda g, p, q: (g, q, p, 0)))
    k_spec = pl.BlockSpec((nb, S, PAIR), im(lambda g, p, q: (g, 0, p)))
    v_spec = pl.BlockSpec((nb, nkb, PAIR, BLK), im(lambda g, p, q: (g, 0, p, 0)))
    o_spec = pl.BlockSpec((nb, PAIR, BLK), im(lambda g, p, q: (g, p, q)))
    return q_spec, k_spec, v_spec, o_spec


def _sb_kernel(qT_ref, k_ref, vT_ref, us_ref, oT_ref, q2_ref, acc_ref, r_ref, *, nb):
    qi = pl.program_id(2)
    chains = _chains(nb)
    _stage_pair_queries(qT_ref, q2_ref, nb)
    acc_ref[...] = jnp.zeros_like(acc_ref)
    r_ref[...] = jnp.zeros_like(r_ref)

    def process(kbs, masked):
        if masked:
            row = lax.broadcasted_iota(jnp.int32, (BLK, BLK), 0)
            col = lax.broadcasted_iota(jnp.int32, (BLK, BLK), 1)
            past = row < col
        tiles = [(kb, c, b, hh) for kb in kbs for c, (b, hh) in enumerate(chains)]
        n_tiles = len(tiles)
        zs, ys, ps, css, pvs = {}, {}, {}, {}, {}
        lag_p, lag_w = SB_LAG
        for step in range(n_tiles + lag_w):
            if step < n_tiles:
                kb, c, b, hh = tiles[step]
                zs[step] = jnp.dot(k_ref[b, pl.ds(pl.multiple_of(kb * BLK, BLK), BLK), :], q2_ref[c],
                                   preferred_element_type=F32)
            i = step - lag_p
            if 0 <= i < n_tiles:
                z = zs.pop(i)
                p = jnp.maximum(jnp.log(1.0 + jnp.exp2(jnp.minimum(z, EXP2_CLAMP))) * LOG2E, z)
                if masked:
                    p = jnp.where(past, p, 0.0)
                ys[i] = z - p
                ps[i] = p.astype(BF16)
                css[i] = jnp.dot(us_ref[...], ps[i], preferred_element_type=F32)
            j = step - lag_w
            if 0 <= j < n_tiles:
                kb, c, b, hh = tiles[j]
                w = jnp.exp2(ys.pop(j) - css[j])
                if masked:
                    w = jnp.where(past, w, 0.0)
                pvs[j] = jnp.dot(vT_ref[b, kb, hh * HEAD_DIM:(hh + 1) * HEAD_DIM, :], w.astype(BF16),
                                 preferred_element_type=F32)
        for c in range(len(chains)):
            r = r_ref[c]
            acc = acc_ref[c]
            for t, (kb, tc, b, hh) in enumerate(tiles):
                if tc == c:
                    acc = acc + pvs[t] * jnp.exp2(-r)
                    r = r + css[t][0:1, :] + ps[t][0:1, :].astype(F32)
            acc_ref[c] = acc
            r_ref[c] = r

    process([qi], True)
    rem = qi % SB_GROUP

    def single(i, carry):
        process([qi - 1 - i], False)
        return carry

    def group(j, carry):
        first = qi - 1 - rem - SB_GROUP * j
        process([first - u for u in range(SB_GROUP)], False)
        return carry

    lax.fori_loop(0, rem, single, 0)
    lax.fori_loop(0, qi // SB_GROUP, group, 0)
    for c, (b, hh) in enumerate(chains):
        oT_ref[b, hh * HEAD_DIM:(hh + 1) * HEAD_DIM, :] = acc_ref[c].astype(BF16)


def _stick_breaking(qT, k, vT, us):
    B, nkb, width, _ = qT.shape
    S = nkb * BLK
    nb = min(ATTN_BATCH, B)
    assert B % nb == 0
    q_spec, k_spec, v_spec, o_spec = _attn_specs(nb, S, nkb)
    nc = 2 * nb
    return pl.pallas_call(
        functools.partial(_sb_kernel, nb=nb),
        grid=(B // nb, N_HEADS // 2, nkb),
        in_specs=[q_spec, k_spec, v_spec, pl.BlockSpec((BLK, BLK), lambda g, p, q: (0, 0))],
        out_specs=o_spec,
        out_shape=jax.ShapeDtypeStruct((B, width, S), BF16),
        scratch_shapes=[pltpu.VMEM((nc, PAIR, BLK), BF16),
                        pltpu.VMEM((nc, HEAD_DIM, BLK), F32),
                        pltpu.VMEM((nc, 1, BLK), F32)],
        compiler_params=pltpu.CompilerParams(
            dimension_semantics=("arbitrary", "arbitrary", "arbitrary"),
            vmem_limit_bytes=VMEM_LIMIT_BYTES),
        name="stickbreak",
    )(qT, k, vT, us)


def _bias_kernel(table_ref, bucket_ref, bias_ref):
    h = pl.program_id(0)
    for t in range(2):
        bk = bucket_ref[t]
        out = jnp.full(bk.shape, NEG_INF, F32)
        for i in range(REL_BUCKETS):
            out = jnp.where(bk == i, table_ref[i * N_HEADS + h] * LOG2E, out)
        bias_ref[0, t] = out


def _t5_bucket_np(dist):
    n = np.maximum(dist, 0)
    max_exact = REL_BUCKETS // 2
    nf = np.maximum(n, 1).astype(np.float64)
    large = max_exact + (np.log(nf / max_exact) / math.log(REL_MAX_DIST / max_exact)
                         * (REL_BUCKETS - max_exact)).astype(np.int32)
    large = np.minimum(large, REL_BUCKETS - 1)
    return np.where(n < max_exact, n, large).astype(np.int32)


def _bias_tiles(rel_table):
    s = np.arange(BLK)[:, None]
    t = np.arange(BLK)[None, :]
    own = np.where(t - s >= 0, _t5_bucket_np(t - s), -1)
    buckets = np.stack([own, _t5_bucket_np(t - s + BLK)]).astype(np.int32)
    return pl.pallas_call(
        _bias_kernel,
        grid_spec=pltpu.PrefetchScalarGridSpec(
            num_scalar_prefetch=1,
            grid=(N_HEADS,),
            in_specs=[pl.BlockSpec((2, BLK, BLK), lambda h, tab: (0, 0, 0))],
            out_specs=pl.BlockSpec((1, 2, BLK, BLK), lambda h, tab: (h, 0, 0, 0)),
        ),
        out_shape=jax.ShapeDtypeStruct((N_HEADS, 2, BLK, BLK), F32),
        name="moba_bias",
    )(rel_table.reshape(-1), jnp.asarray(buckets))


def _moba_kernel(far_ref, qT_ref, k_ref, vT_ref, bias_ref, oT_ref,
                 q2_ref, kmean_ref, sel_ref, acc_ref, m_ref, *, nb, nkb, gate_scale):
    pair = pl.program_id(1)
    own = pl.program_id(2)
    chains = _chains(nb)
    _stage_pair_queries(qT_ref, q2_ref, nb)

    @pl.when(own == 0)
    def _():
        for b in range(nb):
            for n in range(nkb):
                kb = k_ref[b, n * BLK:(n + 1) * BLK, :].astype(F32)
                km = jnp.mean(kb, axis=0, keepdims=True)
                for part in range(3):
                    piece = km.astype(BF16)
                    kmean_ref[b, part * nkb + n:part * nkb + n + 1, :] = piece
                    km = km - piece.astype(F32)

    n_ok = jnp.minimum(own, MOBA_TOPK)
    for c, (b, hh) in enumerate(chains):
        parts = jnp.dot(kmean_ref[b], q2_ref[c], preferred_element_type=F32)
        gate = (parts[:nkb] + parts[nkb:2 * nkb] + parts[2 * nkb:]) * gate_scale
        blk = lax.broadcasted_iota(jnp.int32, gate.shape, 0)
        g = jnp.where(blk < own, gate, NEG_INF)
        sel = jnp.zeros(gate.shape, F32)
        for j in range(min(MOBA_TOPK, nkb)):
            top = jnp.max(g, axis=0, keepdims=True)
            idx = jnp.min(jnp.where(g == top, blk, nkb), axis=0, keepdims=True)
            pick = blk == idx
            sel = jnp.where(pick & (j < n_ok), 1.0, sel)
            g = jnp.where(pick, -jnp.inf, g)
        sel_ref[c] = sel

    m_ref[...] = jnp.full(m_ref.shape, NEG_INF, F32)
    acc_ref[...] = jnp.zeros_like(acc_ref)
    far = [far_ref[2 * pair + hh] * LOG2E for hh in range(2)]
    ones = jnp.ones((SUM_ROWS, BLK), BF16)

    def attend(blocks):
        zcs = [[jnp.dot(k_ref[b, pl.ds(pl.multiple_of(n * BLK, BLK), BLK), :], q2_ref[c],
                        preferred_element_type=F32) for n, kind in blocks]
               for c, (b, hh) in enumerate(chains)]
        pss, alphas = [], []
        for c, (b, hh) in enumerate(chains):
            m_old = m_ref[c]
            m_new = m_old
            terms = []
            for (n, kind), z in zip(blocks, zcs[c]):
                if kind == "own":
                    s, shift, chosen = z + bias_ref[hh, 0], 0.0, None
                elif kind == "prev":
                    s, shift, chosen = z + bias_ref[hh, 1], 0.0, sel_ref[c, pl.ds(n, 1), :] > 0.0
                else:
                    s, shift, chosen = z, far[hh], sel_ref[c, pl.ds(n, 1), :] > 0.0
                m_blk = jnp.max(s, axis=0, keepdims=True) + shift
                if chosen is not None:
                    m_blk = jnp.where(chosen, m_blk, NEG_INF)
                m_new = jnp.maximum(m_new, m_blk)
                terms.append((s, shift, chosen))
            alphas.append(jnp.exp2(m_old - m_new))
            ps = []
            for s, shift, chosen in terms:
                sub = m_new - shift
                if chosen is not None:
                    sub = jnp.where(chosen, sub, -NEG_INF)
                ps.append(jnp.exp2(s - sub).astype(BF16))
            m_ref[c] = m_new
            pss.append(ps)
        for c, (b, hh) in enumerate(chains):
            acc = alphas[c] * acc_ref[c]
            for (n, kind), p in zip(blocks, pss[c]):
                v1 = jnp.concatenate([vT_ref[b, n, hh * HEAD_DIM:(hh + 1) * HEAD_DIM, :], ones], axis=0)
                acc = acc + jnp.dot(v1, p, preferred_element_type=F32)
            acc_ref[c] = acc

    attend([(own, "own"), (jnp.maximum(own - 1, 0), "prev")])
    n_far = jnp.maximum(own - 1, 0)
    rem = n_far % MOBA_GROUP

    def single(i, carry):
        attend([(i, "far")])
        return carry

    def group(j, carry):
        attend([(rem + MOBA_GROUP * j + u, "far") for u in range(MOBA_GROUP)])
        return carry

    lax.fori_loop(0, rem, single, 0)
    lax.fori_loop(0, n_far // MOBA_GROUP, group, 0)
    for c, (b, hh) in enumerate(chains):
        acc = acc_ref[c]
        oT_ref[b, hh * HEAD_DIM:(hh + 1) * HEAD_DIM, :] = (
            acc[:HEAD_DIM] / acc[HEAD_DIM:HEAD_DIM + 1]).astype(BF16)


def _moba(qT, k, vT, bias, far, *, gate_scale):
    B, nkb, width, _ = qT.shape
    S = nkb * BLK
    nb = min(ATTN_BATCH, B)
    assert B % nb == 0
    q_spec, k_spec, v_spec, o_spec = _attn_specs(nb, S, nkb, extra_args=1)
    nc = 2 * nb
    kern = functools.partial(_moba_kernel, nb=nb, nkb=nkb, gate_scale=gate_scale)
    return pl.pallas_call(
        kern,
        grid_spec=pltpu.PrefetchScalarGridSpec(
            num_scalar_prefetch=1,
            grid=(B // nb, N_HEADS // 2, nkb),
            in_specs=[q_spec, k_spec, v_spec,
                      pl.BlockSpec((2, 2, BLK, BLK), lambda g, p, q, far: (p, 0, 0, 0))],
            out_specs=o_spec,
            scratch_shapes=[pltpu.VMEM((nc, PAIR, BLK), BF16),
                            pltpu.VMEM((nb, 3 * nkb, PAIR), BF16),
                            pltpu.VMEM((nc, nkb, BLK), F32),
                            pltpu.VMEM((nc, HEAD_DIM + SUM_ROWS, BLK), F32),
                            pltpu.VMEM((nc, 1, BLK), F32)],
        ),
        out_shape=jax.ShapeDtypeStruct((B, width, S), BF16),
        compiler_params=pltpu.CompilerParams(
            dimension_semantics=("arbitrary", "arbitrary", "arbitrary"),
            vmem_limit_bytes=VMEM_LIMIT_BYTES),
        name="moba",
    )(far, qT, k, vT, bias)


def _out_kernel(oT_ref, gT_ref, x_ref, woT_ref, lng_ref, lnb_ref, h_ref, yT_ref):
    g = gT_ref[0].astype(F32)
    og = (oT_ref[0].astype(F32) * (g * (1.0 / (1.0 + jnp.exp(-g))))).astype(BF16)
    yT_ref[...] = jnp.dot(woT_ref[...], og, preferred_element_type=F32)
    r = DEEPNORM_ALPHA * x_ref[0] + yT_ref[...].T
    mu = jnp.mean(r, axis=-1, keepdims=True)
    d = r - mu
    var = jnp.mean(d * d, axis=-1, keepdims=True)
    h_ref[0] = d * lax.rsqrt(var + LN_EPS) * lng_ref[...] + lnb_ref[...]


def _out_proj_ln(oT, gT, x, woT, lng, lnb):
    B, S, D = x.shape
    width = oT.shape[1]
    rows = min(PROJ_ROWS, S)
    return pl.pallas_call(
        _out_kernel,
        grid=(B, S // rows),
        in_specs=[
            pl.BlockSpec((1, width, rows), lambda b, s: (b, 0, s)),
            pl.BlockSpec((1, width, rows), lambda b, s: (b, 0, s)),
            pl.BlockSpec((1, rows, D), lambda b, s: (b, s, 0)),
            pl.BlockSpec((D, width), lambda b, s: (0, 0)),
            pl.BlockSpec((1, D), lambda b, s: (0, 0)),
            pl.BlockSpec((1, D), lambda b, s: (0, 0)),
        ],
        out_specs=pl.BlockSpec((1, rows, D), lambda b, s: (b, s, 0)),
        out_shape=jax.ShapeDtypeStruct((B, S, D), F32),
        scratch_shapes=[pltpu.VMEM((D, rows), F32)],
        compiler_params=pltpu.CompilerParams(
            dimension_semantics=("arbitrary", "arbitrary"), vmem_limit_bytes=VMEM_LIMIT_BYTES),
        name="outproj_ln",
    )(oT, gT, x, woT, lng, lnb)


def _suffix_sum_matrix():
    return jnp.asarray(np.triu(np.ones((BLK, BLK), np.float32), k=1), BF16)


def kernel(x, w_in, w_out, ln_g, ln_b, rel_table):
    width = N_HEADS * HEAD_DIM
    scale = HEAD_DIM ** -0.5 * LOG2E
    us = _suffix_sum_matrix()
    bias = _bias_tiles(rel_table)
    far = rel_table[REL_BUCKETS - 1]
    h = x
    for layer in range(DEPTH):
        w = w_in[layer]
        wk = w[:, width:2 * width].astype(BF16)
        wt = jnp.concatenate([w[:, :width], w[:, 2 * width:]], axis=1).T.astype(BF16)
        woT = w_out[layer].T.astype(BF16)
        k, qT, vT, gT = _project(h, wk, wt, width=width, scale=scale)
        if layer % 2 == 0:
            oT = _stick_breaking(qT, k, vT, us)
        else:
            oT = _moba(qT, k, vT, bias, far, gate_scale=1.0 / scale)
        h = _out_proj_ln(oT, gT, h, woT, ln_g[layer][None, :], ln_b[layer][None, :])
    return h
```

```python
import functools
import math

import jax
import jax.numpy as jnp
import numpy as np
from jax import lax
from jax.experimental import pallas as pl
from jax.experimental.pallas import tpu as pltpu

N_HEADS = 16
HEAD_DIM = 64
DEPTH = 2
MOBA_BLOCK = 256
MOBA_TOPK = 3
REL_BUCKETS = 32
REL_MAX_DIST = 128
LN_EPS = 1e-5
DEEPNORM_ALPHA = (2.0 * DEPTH) ** 0.25
NEG_INF = -1e30
LOG2E = math.log2(math.e)
EXP2_CLAMP = 100.0

BLK = 256
PROJ_ROWS = 512
ATTN_BATCH = 4
SB_GROUP = 3
SB_LAG = (2, 4)
MOBA_GROUP = 2
SUM_ROWS = 16
FIXED_MAX_MARGIN = 60.0
NORM_SLACK = 1.0 + 2.0 ** -6
VMEM_LIMIT_BYTES = 56 * 1024 * 1024

F32 = jnp.float32
BF16 = jnp.bfloat16


def _proj_kernel(x_ref, wk_ref, wt_ref, k_ref, qT_ref, vT_ref, gT_ref, *, width, scale):
    xb = x_ref[0].astype(BF16)
    k_ref[0] = jnp.dot(xb, wk_ref[...], preferred_element_type=F32).astype(BF16)
    nt = (((1,), (1,)), ((), ()))
    n_blk = xb.shape[0] // BLK

    def t_part(i):
        return lax.dot_general(wt_ref[i * width:(i + 1) * width, :], xb, nt,
                               preferred_element_type=F32)

    qT = (t_part(0) * scale).astype(BF16)
    vT = t_part(1).astype(BF16)
    for j in range(n_blk):
        qT_ref[0, j] = qT[:, j * BLK:(j + 1) * BLK]
        vT_ref[0, j] = vT[:, j * BLK:(j + 1) * BLK]
    gT_ref[0] = t_part(2).astype(BF16)


def _project(x, wk, wt, *, width, scale):
    B, S, D = x.shape
    rows = min(PROJ_ROWS, S)
    n_blk = rows // BLK
    nkb = S // BLK
    kern = functools.partial(_proj_kernel, width=width, scale=scale)
    return pl.pallas_call(
        kern,
        grid=(B, S // rows),
        in_specs=[
            pl.BlockSpec((1, rows, D), lambda b, s: (b, s, 0)),
            pl.BlockSpec((D, width), lambda b, s: (0, 0)),
            pl.BlockSpec((3 * width, D), lambda b, s: (0, 0)),
        ],
        out_specs=[
            pl.BlockSpec((1, rows, width), lambda b, s: (b, s, 0)),
            pl.BlockSpec((1, n_blk, width, BLK), lambda b, s: (b, s, 0, 0)),
            pl.BlockSpec((1, n_blk, width, BLK), lambda b, s: (b, s, 0, 0)),
            pl.BlockSpec((1, width, rows), lambda b, s: (b, 0, s)),
        ],
        out_shape=[
            jax.ShapeDtypeStruct((B, S, width), BF16),
            jax.ShapeDtypeStruct((B, nkb, width, BLK), BF16),
            jax.ShapeDtypeStruct((B, nkb, width, BLK), BF16),
            jax.ShapeDtypeStruct((B, width, S), BF16),
        ],
        compiler_params=pltpu.CompilerParams(
            dimension_semantics=("arbitrary", "arbitrary"), vmem_limit_bytes=VMEM_LIMIT_BYTES),
        name="proj",
    )(x, wk, wt)


PAIR = 2 * HEAD_DIM


def _chains(nb):
    return [(b, hh) for b in range(nb) for hh in range(2)]


def _stage_pair_queries(qT_ref, q2_ref, nb):
    zeros = jnp.zeros((HEAD_DIM, BLK), BF16)
    for c, (b, hh) in enumerate(_chains(nb)):
        q = qT_ref[b, 0, hh * HEAD_DIM:(hh + 1) * HEAD_DIM, :]
        q2_ref[c] = jnp.concatenate([q, zeros] if hh == 0 else [zeros, q], axis=0)


def _attn_specs(nb, S, nkb, extra_args=0):
    def im(f):
        if extra_args:
            return lambda g, p, q, *_: f(g, p, q)
        return f
    q_spec = pl.BlockSpec((nb, 1, PAIR, BLK), im(lambda g, p, q: (g, q, p, 0)))
    k_spec = pl.BlockSpec((nb, S, PAIR), im(lambda g, p, q: (g, 0, p)))
    v_spec = pl.BlockSpec((nb, nkb, PAIR, BLK), im(lambda g, p, q: (g, 0, p, 0)))
    o_spec = pl.BlockSpec((nb, PAIR, BLK), im(lambda g, p, q: (g, p, q)))
    return q_spec, k_spec, v_spec, o_spec


def _sb_kernel(qT_ref, k_ref, vT_ref, us_ref, oT_ref, q2_ref, acc_ref, r_ref, *, nb):
    qi = pl.program_id(2)
    chains = _chains(nb)
    _stage_pair_queries(qT_ref, q2_ref, nb)
    acc_ref[...] = jnp.zeros_like(acc_ref)
    r_ref[...] = jnp.zeros_like(r_ref)

    def process(kbs, masked):
        if masked:
            row = lax.broadcasted_iota(jnp.int32, (BLK, BLK), 0)
            col = lax.broadcasted_iota(jnp.int32, (BLK, BLK), 1)
            past = row < col
        tiles = [(kb, c, b, hh) for kb in kbs for c, (b, hh) in enumerate(chains)]
        n_tiles = len(tiles)
        zs, ys, ps, css, pvs = {}, {}, {}, {}, {}
        lag_p, lag_w = SB_LAG
        for step in range(n_tiles + lag_w):
            if step < n_tiles:
                kb, c, b, hh = tiles[step]
                zs[step] = jnp.dot(k_ref[b, pl.ds(pl.multiple_of(kb * BLK, BLK), BLK), :], q2_ref[c],
                                   preferred_element_type=F32)
            i = step - lag_p
            if 0 <= i < n_tiles:
                z = zs.pop(i)
                p = jnp.maximum(jnp.log(1.0 + jnp.exp2(jnp.minimum(z, EXP2_CLAMP))) * LOG2E, z)
                if masked:
                    p = jnp.where(past, p, 0.0)
                ys[i] = z - p
                ps[i] = p.astype(BF16)
                css[i] = jnp.dot(us_ref[...], ps[i], preferred_element_type=F32)
            j = step - lag_w
            if 0 <= j < n_tiles:
                kb, c, b, hh = tiles[j]
                w = jnp.exp2(ys.pop(j) - css[j])
                if masked:
                    w = jnp.where(past, w, 0.0)
                pvs[j] = jnp.dot(vT_ref[b, kb, hh * HEAD_DIM:(hh + 1) * HEAD_DIM, :], w.astype(BF16),
                                 preferred_element_type=F32)
        for c in range(len(chains)):
            r = r_ref[c]
            acc = acc_ref[c]
            for t, (kb, tc, b, hh) in enumerate(tiles):
                if tc == c:
                    acc = acc + pvs[t] * jnp.exp2(-r)
                    r = r + css[t][0:1, :] + ps[t][0:1, :].astype(F32)
            acc_ref[c] = acc
            r_ref[c] = r

    process([qi], True)
    rem = qi % SB_GROUP

    def single(i, carry):
        process([qi - 1 - i], False)
        return carry

    def group(j, carry):
        first = qi - 1 - rem - SB_GROUP * j
        process([first - u for u in range(SB_GROUP)], False)
        return carry

    lax.fori_loop(0, rem, single, 0)
    lax.fori_loop(0, qi // SB_GROUP, group, 0)
    for c, (b, hh) in enumerate(chains):
        oT_ref[b, hh * HEAD_DIM:(hh + 1) * HEAD_DIM, :] = acc_ref[c].astype(BF16)


def _stick_breaking(qT, k, vT, us):
    B, nkb, width, _ = qT.shape
    S = nkb * BLK
    nb = min(ATTN_BATCH, B)
    assert B % nb == 0
    q_spec, k_spec, v_spec, o_spec = _attn_specs(nb, S, nkb)
    nc = 2 * nb
    return pl.pallas_call(
        functools.partial(_sb_kernel, nb=nb),
        grid=(B // nb, N_HEADS // 2, nkb),
        in_specs=[q_spec, k_spec, v_spec, pl.BlockSpec((BLK, BLK), lambda g, p, q: (0, 0))],
        out_specs=o_spec,
        out_shape=jax.ShapeDtypeStruct((B, width, S), BF16),
        scratch_shapes=[pltpu.VMEM((nc, PAIR, BLK), BF16),
                        pltpu.VMEM((nc, HEAD_DIM, BLK), F32),
                        pltpu.VMEM((nc, 1, BLK), F32)],
        compiler_params=pltpu.CompilerParams(
            dimension_semantics=("arbitrary", "arbitrary", "arbitrary"),
            vmem_limit_bytes=VMEM_LIMIT_BYTES),
        name="stickbreak",
    )(qT, k, vT, us)


def _bias_kernel(table_ref, bucket_ref, bias_ref):
    h = pl.program_id(0)
    for t in range(2):
        bk = bucket_ref[t]
        out = jnp.full(bk.shape, NEG_INF, F32)
        for i in range(REL_BUCKETS):
            out = jnp.where(bk == i, table_ref[i * N_HEADS + h] * LOG2E, out)
        bias_ref[0, t] = out


def _t5_bucket_np(dist):
    n = np.maximum(dist, 0)
    max_exact = REL_BUCKETS // 2
    nf = np.maximum(n, 1).astype(np.float64)
    large = max_exact + (np.log(nf / max_exact) / math.log(REL_MAX_DIST / max_exact)
                         * (REL_BUCKETS - max_exact)).astype(np.int32)
    large = np.minimum(large, REL_BUCKETS - 1)
    return np.where(n < max_exact, n, large).astype(np.int32)


def _bias_tiles(rel_table):
    s = np.arange(BLK)[:, None]
    t = np.arange(BLK)[None, :]
    own = np.where(t - s >= 0, _t5_bucket_np(t - s), -1)
    buckets = np.stack([own, _t5_bucket_np(t - s + BLK)]).astype(np.int32)
    return pl.pallas_call(
        _bias_kernel,
        grid_spec=pltpu.PrefetchScalarGridSpec(
            num_scalar_prefetch=1,
            grid=(N_HEADS,),
            in_specs=[pl.BlockSpec((2, BLK, BLK), lambda h, tab: (0, 0, 0))],
            out_specs=pl.BlockSpec((1, 2, BLK, BLK), lambda h, tab: (h, 0, 0, 0)),
        ),
        out_shape=jax.ShapeDtypeStruct((N_HEADS, 2, BLK, BLK), F32),
        name="moba_bias",
    )(rel_table.reshape(-1), jnp.asarray(buckets))


def _moba_kernel(far_ref, qT_ref, k_ref, vT_ref, bias_ref, oT_ref,
                 q2_ref, kmean_ref, knorm_ref, sel_ref, acc_ref, m_ref, *, nb, nkb, gate_scale):
    pair = pl.program_id(1)
    own = pl.program_id(2)
    chains = _chains(nb)
    _stage_pair_queries(qT_ref, q2_ref, nb)

    @pl.when(own == 0)
    def _():
        for b in range(nb):
            for n in range(nkb):
                kb = k_ref[b, n * BLK:(n + 1) * BLK, :].astype(F32)
                km = jnp.mean(kb, axis=0, keepdims=True)
                for part in range(3):
                    piece = km.astype(BF16)
                    kmean_ref[b, part * nkb + n:part * nkb + n + 1, :] = piece
                    km = km - piece.astype(F32)
        lane = lax.broadcasted_iota(jnp.int32, (PAIR, PAIR), 0) // HEAD_DIM
        same_head = (lane == lax.broadcasted_iota(jnp.int32, (PAIR, PAIR), 1) // HEAD_DIM).astype(BF16)
        for b in range(nb):
            kf = k_ref[b].astype(F32)
            norm2 = jnp.max(jnp.dot((kf * kf).astype(BF16), same_head, preferred_element_type=F32),
                            axis=0, keepdims=True)
            for hh in range(2):
                knorm_ref[2 * b + hh] = jnp.broadcast_to(
                    jnp.sqrt(norm2[:, hh * HEAD_DIM:hh * HEAD_DIM + 1]), (1, BLK))

    n_ok = jnp.minimum(own, MOBA_TOPK)
    for c, (b, hh) in enumerate(chains):
        parts = jnp.dot(kmean_ref[b], q2_ref[c], preferred_element_type=F32)
        gate = (parts[:nkb] + parts[nkb:2 * nkb] + parts[2 * nkb:]) * gate_scale
        blk = lax.broadcasted_iota(jnp.int32, gate.shape, 0)
        g = jnp.where(blk < own, gate, NEG_INF)
        sel = jnp.zeros(gate.shape, F32)
        for j in range(min(MOBA_TOPK, nkb)):
            top = jnp.max(g, axis=0, keepdims=True)
            idx = jnp.min(jnp.where(g == top, blk, nkb), axis=0, keepdims=True)
            pick = blk == idx
            sel = jnp.where(pick & (j < n_ok), 1.0, sel)
            g = jnp.where(pick, -jnp.inf, g)
        sel_ref[c] = sel

    m_ref[...] = jnp.full(m_ref.shape, NEG_INF, F32)
    acc_ref[...] = jnp.zeros_like(acc_ref)
    far = [far_ref[2 * pair + hh] * LOG2E for hh in range(2)]
    ones = jnp.ones((SUM_ROWS, BLK), BF16)

    def attend(blocks):
        zcs = [[jnp.dot(k_ref[b, pl.ds(pl.multiple_of(n * BLK, BLK), BLK), :], q2_ref[c],
                        preferred_element_type=F32) for n, kind in blocks]
               for c, (b, hh) in enumerate(chains)]
        pss, alphas = [], []
        for c, (b, hh) in enumerate(chains):
            m_old = m_ref[c]
            m_new = m_old
            terms = []
            for (n, kind), z in zip(blocks, zcs[c]):
                if kind == "own":
                    s, shift, chosen = z + bias_ref[hh, 0], 0.0, None
                elif kind == "prev":
                    s, shift, chosen = z + bias_ref[hh, 1], 0.0, sel_ref[c, pl.ds(n, 1), :] > 0.0
                else:
                    s, shift, chosen = z, far[hh], sel_ref[c, pl.ds(n, 1), :] > 0.0
                m_blk = jnp.max(s, axis=0, keepdims=True) + shift
                if chosen is not None:
                    m_blk = jnp.where(chosen, m_blk, NEG_INF)
                m_new = jnp.maximum(m_new, m_blk)
                terms.append((s, shift, chosen))
            alphas.append(jnp.exp2(m_old - m_new))
            ps = []
            for s, shift, chosen in terms:
                sub = m_new - shift
                if chosen is not None:
                    sub = jnp.where(chosen, sub, -NEG_INF)
                ps.append(jnp.exp2(s - sub).astype(BF16))
            m_ref[c] = m_new
            pss.append(ps)
        for c, (b, hh) in enumerate(chains):
            acc = alphas[c] * acc_ref[c]
            for (n, kind), p in zip(blocks, pss[c]):
                v1 = jnp.concatenate([vT_ref[b, n, hh * HEAD_DIM:(hh + 1) * HEAD_DIM, :], ones], axis=0)
                acc = acc + jnp.dot(v1, p, preferred_element_type=F32)
            acc_ref[c] = acc

    def attend_fixed(ns):
        zcs = [[jnp.dot(k_ref[b, pl.ds(pl.multiple_of(n * BLK, BLK), BLK), :], q2_ref[c],
                        preferred_element_type=F32) for n in ns]
               for c, (b, hh) in enumerate(chains)]
        pss = []
        for c, (b, hh) in enumerate(chains):
            ref = m_ref[c] - far[hh]
            ps = []
            for n, z in zip(ns, zcs[c]):
                chosen = sel_ref[c, pl.ds(n, 1), :] > 0.0
                ps.append(jnp.exp2(z - jnp.where(chosen, ref, -NEG_INF)).astype(BF16))
            pss.append(ps)
        for c, (b, hh) in enumerate(chains):
            acc = acc_ref[c]
            for n, p in zip(ns, pss[c]):
                v1 = jnp.concatenate([vT_ref[b, n, hh * HEAD_DIM:(hh + 1) * HEAD_DIM, :], ones], axis=0)
                acc = acc + jnp.dot(v1, p, preferred_element_type=F32)
            acc_ref[c] = acc

    attend([(own, "own"), (jnp.maximum(own - 1, 0), "prev")])
    n_far = jnp.maximum(own - 1, 0)
    rem = n_far % MOBA_GROUP

    worst = jnp.zeros((1, BLK), F32)
    for c, (b, hh) in enumerate(chains):
        q = q2_ref[c].astype(F32)
        qnorm = jnp.sqrt(jnp.sum(q * q, axis=0, keepdims=True))
        bound = knorm_ref[c] * qnorm * NORM_SLACK + far[hh]
        worst = jnp.maximum(worst, jnp.where(bound <= m_ref[c] + FIXED_MAX_MARGIN, 0.0, 1.0))
    fixed_ok = jnp.max(worst) == 0.0

    def far_loops(step):
        def single(i, carry):
            step([i])
            return carry

        def group(j, carry):
            step([rem + MOBA_GROUP * j + u for u in range(MOBA_GROUP)])
            return carry

        lax.fori_loop(0, rem, single, 0)
        lax.fori_loop(0, n_far // MOBA_GROUP, group, 0)

    @pl.when(fixed_ok)
    def _():
        far_loops(attend_fixed)

    @pl.when(jnp.logical_not(fixed_ok))
    def _():
        far_loops(lambda ns: attend([(n, "far") for n in ns]))
    for c, (b, hh) in enumerate(chains):
        acc = acc_ref[c]
        oT_ref[b, hh * HEAD_DIM:(hh + 1) * HEAD_DIM, :] = (
            acc[:HEAD_DIM] / acc[HEAD_DIM:HEAD_DIM + 1]).astype(BF16)


def _moba(qT, k, vT, bias, far, *, gate_scale):
    B, nkb, width, _ = qT.shape
    S = nkb * BLK
    nb = min(ATTN_BATCH, B)
    assert B % nb == 0
    q_spec, k_spec, v_spec, o_spec = _attn_specs(nb, S, nkb, extra_args=1)
    nc = 2 * nb
    kern = functools.partial(_moba_kernel, nb=nb, nkb=nkb, gate_scale=gate_scale)
    return pl.pallas_call(
        kern,
        grid_spec=pltpu.PrefetchScalarGridSpec(
            num_scalar_prefetch=1,
            grid=(B // nb, N_HEADS // 2, nkb),
            in_specs=[q_spec, k_spec, v_spec,
                      pl.BlockSpec((2, 2, BLK, BLK), lambda g, p, q, far: (p, 0, 0, 0))],
            out_specs=o_spec,
            scratch_shapes=[pltpu.VMEM((nc, PAIR, BLK), BF16),
                            pltpu.VMEM((nb, 3 * nkb, PAIR), BF16),
                            pltpu.VMEM((nc, 1, BLK), F32),
                            pltpu.VMEM((nc, nkb, BLK), F32),
                            pltpu.VMEM((nc, HEAD_DIM + SUM_ROWS, BLK), F32),
                            pltpu.VMEM((nc, 1, BLK), F32)],
        ),
        out_shape=jax.ShapeDtypeStruct((B, width, S), BF16),
        compiler_params=pltpu.CompilerParams(
            dimension_semantics=("arbitrary", "arbitrary", "arbitrary"),
            vmem_limit_bytes=VMEM_LIMIT_BYTES),
        name="moba",
    )(far, qT, k, vT, bias)


def _out_kernel(oT_ref, gT_ref, x_ref, woT_ref, lng_ref, lnb_ref, h_ref, yT_ref):
    g = gT_ref[0].astype(F32)
    og = (oT_ref[0].astype(F32) * (g * (1.0 / (1.0 + jnp.exp(-g))))).astype(BF16)
    yT_ref[...] = jnp.dot(woT_ref[...], og, preferred_element_type=F32)
    r = DEEPNORM_ALPHA * x_ref[0] + yT_ref[...].T
    mu = jnp.mean(r, axis=-1, keepdims=True)
    d = r - mu
    var = jnp.mean(d * d, axis=-1, keepdims=True)
    h_ref[0] = d * lax.rsqrt(var + LN_EPS) * lng_ref[...] + lnb_ref[...]


def _out_proj_ln(oT, gT, x, woT, lng, lnb):
    B, S, D = x.shape
    width = oT.shape[1]
    rows = min(PROJ_ROWS, S)
    return pl.pallas_call(
        _out_kernel,
        grid=(B, S // rows),
        in_specs=[
            pl.BlockSpec((1, width, rows), lambda b, s: (b, 0, s)),
            pl.BlockSpec((1, width, rows), lambda b, s: (b, 0, s)),
            pl.BlockSpec((1, rows, D), lambda b, s: (b, s, 0)),
            pl.BlockSpec((D, width), lambda b, s: (0, 0)),
            pl.BlockSpec((1, D), lambda b, s: (0, 0)),
            pl.BlockSpec((1, D), lambda b, s: (0, 0)),
        ],
        out_specs=pl.BlockSpec((1, rows, D), lambda b, s: (b, s, 0)),
        out_shape=jax.ShapeDtypeStruct((B, S, D), F32),
        scratch_shapes=[pltpu.VMEM((D, rows), F32)],
        compiler_params=pltpu.CompilerParams(
            dimension_semantics=("arbitrary", "arbitrary"), vmem_limit_bytes=VMEM_LIMIT_BYTES),
        name="outproj_ln",
    )(oT, gT, x, woT, lng, lnb)


def _suffix_sum_matrix():
    return jnp.asarray(np.triu(np.ones((BLK, BLK), np.float32), k=1), BF16)


def kernel(x, w_in, w_out, ln_g, ln_b, rel_table):
    width = N_HEADS * HEAD_DIM
    scale = HEAD_DIM ** -0.5 * LOG2E
    us = _suffix_sum_matrix()
    bias = _bias_tiles(rel_table)
    far = rel_table[REL_BUCKETS - 1]
    h = x
    for layer in range(DEPTH):
        w = w_in[layer]
        wk = w[:, width:2 * width].astype(BF16)
        wt = jnp.concatenate([w[:, :width], w[:, 2 * width:]], axis=1).T.astype(BF16)
        woT = w_out[layer].T.astype(BF16)
        k, qT, vT, gT = _project(h, wk, wt, width=width, scale=scale)
        if layer % 2 == 0:
            oT = _stick_breaking(qT, k, vT, us)
        else:
            oT = _moba(qT, k, vT, bias, far, gate_scale=1.0 / scale)
        h = _out_proj_ln(oT, gT, h, woT, ln_g[layer][None, :], ln_b[layer][None, :])
    return h
```

```python
import functools
import math

import jax
import jax.numpy as jnp
import numpy as np
from jax import lax
from jax.experimental import pallas as pl
from jax.experimental.pallas import tpu as pltpu

N_HEADS = 16
HEAD_DIM = 64
DEPTH = 2
MOBA_BLOCK = 256
MOBA_TOPK = 3
REL_BUCKETS = 32
REL_MAX_DIST = 128
LN_EPS = 1e-5
DEEPNORM_ALPHA = (2.0 * DEPTH) ** 0.25
NEG_INF = -1e30
LOG2E = math.log2(math.e)
EXP2_CLAMP = 100.0

BLK = 256
PROJ_ROWS = 512
ATTN_BATCH = 4
SB_GROUP = 4
SB_TAIL_GROUPS = (2, 1)
SB_LAG = (2, 4)
MOBA_GROUP = 2
SUM_ROWS = 16
FIXED_MAX_MARGIN = 60.0
NORM_SLACK = 1.0 + 2.0 ** -6
VMEM_LIMIT_BYTES = 56 * 1024 * 1024

F32 = jnp.float32
BF16 = jnp.bfloat16


def _proj_kernel(x_ref, w_ref, k_ref, qT_ref, vT_ref, gT_ref, q_scr, v_scr, g_scr, *, width, scale):
    xb = x_ref[0].astype(BF16)
    n_blk = xb.shape[0] // BLK

    def part(i):
        return jnp.dot(xb, w_ref[:, i * width:(i + 1) * width].astype(BF16),
                       preferred_element_type=F32)

    k_ref[0] = part(1).astype(BF16)
    q_scr[...] = part(0) * scale
    v_scr[...] = part(2)
    g_scr[...] = part(3)
    qT = q_scr[...].T.astype(BF16)
    vT = v_scr[...].T.astype(BF16)
    for j in range(n_blk):
        qT_ref[0, j] = qT[:, j * BLK:(j + 1) * BLK]
        vT_ref[0, j] = vT[:, j * BLK:(j + 1) * BLK]
    gT_ref[0] = g_scr[...].T.astype(BF16)


def _project(x, w, *, width, scale):
    B, S, D = x.shape
    rows = min(PROJ_ROWS, S)
    n_blk = rows // BLK
    nkb = S // BLK
    kern = functools.partial(_proj_kernel, width=width, scale=scale)
    return pl.pallas_call(
        kern,
        grid=(B, S // rows),
        in_specs=[
            pl.BlockSpec((1, rows, D), lambda b, s: (b, s, 0)),
            pl.BlockSpec((D, 4 * width), lambda b, s: (0, 0), pipeline_mode=pl.Buffered(1)),
        ],
        out_specs=[
            pl.BlockSpec((1, rows, width), lambda b, s: (b, s, 0)),
            pl.BlockSpec((1, n_blk, width, BLK), lambda b, s: (b, s, 0, 0)),
            pl.BlockSpec((1, n_blk, width, BLK), lambda b, s: (b, s, 0, 0)),
            pl.BlockSpec((1, width, rows), lambda b, s: (b, 0, s)),
        ],
        out_shape=[
            jax.ShapeDtypeStruct((B, S, width), BF16),
            jax.ShapeDtypeStruct((B, nkb, width, BLK), BF16),
            jax.ShapeDtypeStruct((B, nkb, width, BLK), BF16),
            jax.ShapeDtypeStruct((B, width, S), BF16),
        ],
        scratch_shapes=[pltpu.VMEM((rows, width), F32)] * 3,
        compiler_params=pltpu.CompilerParams(
            dimension_semantics=("arbitrary", "arbitrary"), vmem_limit_bytes=VMEM_LIMIT_BYTES),
        name="proj",
    )(x, w)


PAIR = 2 * HEAD_DIM


def _chains(nb):
    return [(b, hh) for b in range(nb) for hh in range(2)]


def _stage_pair_queries(qT_ref, q2_ref, nb):
    zeros = jnp.zeros((HEAD_DIM, BLK), BF16)
    for c, (b, hh) in enumerate(_chains(nb)):
        q = qT_ref[b, 0, hh * HEAD_DIM:(hh + 1) * HEAD_DIM, :]
        q2_ref[c] = jnp.concatenate([q, zeros] if hh == 0 else [zeros, q], axis=0)


def _attn_specs(nb, S, nkb, extra_args=0):
    def im(f):
        if extra_args:
            return lambda g, p, q, *_: f(g, p, q)
        return f
    q_spec = pl.BlockSpec((nb, 1, PAIR, BLK), im(lambda g, p, q: (g, q, p, 0)))
    k_spec = pl.BlockSpec((nb, S, PAIR), im(lambda g, p, q: (g, 0, p)))
    v_spec = pl.BlockSpec((nb, nkb, PAIR, BLK), im(lambda g, p, q: (g, 0, p, 0)))
    o_spec = pl.BlockSpec((nb, PAIR, BLK), im(lambda g, p, q: (g, p, q)))
    return q_spec, k_spec, v_spec, o_spec


def _sb_kernel(qT_ref, k_ref, vT_ref, us_ref, oT_ref, q2_ref, acc_ref, r_ref, *, nb):
    qi = pl.program_id(2)
    chains = _chains(nb)
    _stage_pair_queries(qT_ref, q2_ref, nb)
    acc_ref[...] = jnp.zeros_like(acc_ref)
    r_ref[...] = jnp.zeros_like(r_ref)

    def process(kbs, masked):
        if masked:
            row = lax.broadcasted_iota(jnp.int32, (BLK, BLK), 0)
            col = lax.broadcasted_iota(jnp.int32, (BLK, BLK), 1)
            past = row < col
        tiles = [(kb, c, b, hh) for kb in kbs for c, (b, hh) in enumerate(chains)]
        n_tiles = len(tiles)
        zs, ys, ps, css, pvs = {}, {}, {}, {}, {}
        lag_p, lag_w = SB_LAG
        for step in range(n_tiles + lag_w):
            if step < n_tiles:
                kb, c, b, hh = tiles[step]
                zs[step] = jnp.dot(k_ref[b, pl.ds(pl.multiple_of(kb * BLK, BLK), BLK), :], q2_ref[c],
                                   preferred_element_type=F32)
            i = step - lag_p
            if 0 <= i < n_tiles:
                z = zs.pop(i)
                p = jnp.maximum(jnp.log(1.0 + jnp.exp2(jnp.minimum(z, EXP2_CLAMP))) * LOG2E, z)
                if masked:
                    p = jnp.where(past, p, 0.0)
                ys[i] = z - p
                ps[i] = p.astype(BF16)
                css[i] = jnp.dot(us_ref[...], ps[i], preferred_element_type=F32)
            j = step - lag_w
            if 0 <= j < n_tiles:
                kb, c, b, hh = tiles[j]
                w = jnp.exp2(ys.pop(j) - css[j])
                if masked:
                    w = jnp.where(past, w, 0.0)
                pvs[j] = jnp.dot(vT_ref[b, kb, hh * HEAD_DIM:(hh + 1) * HEAD_DIM, :], w.astype(BF16),
                                 preferred_element_type=F32)
        for c in range(len(chains)):
            r = r_ref[c]
            acc = acc_ref[c]
            for t, (kb, tc, b, hh) in enumerate(tiles):
                if tc == c:
                    acc = acc + pvs[t] * jnp.exp2(-r)
                    r = r + css[t][0:1, :] + ps[t][0:1, :].astype(F32)
            acc_ref[c] = acc
            r_ref[c] = r

    process([qi], True)
    n_big = qi // SB_GROUP
    left = qi - n_big * SB_GROUP

    def group(j, carry):
        first = qi - 1 - SB_GROUP * j
        process([first - u for u in range(SB_GROUP)], False)
        return carry

    lax.fori_loop(0, n_big, group, 0)
    for size in SB_TAIL_GROUPS:
        @pl.when(left // size % 2 == 1)
        def _(size=size):
            first = left % (2 * size) - 1
            process([first - u for u in range(size)], False)
    for c, (b, hh) in enumerate(chains):
        oT_ref[b, hh * HEAD_DIM:(hh + 1) * HEAD_DIM, :] = acc_ref[c].astype(BF16)


def _stick_breaking(qT, k, vT, us):
    B, nkb, width, _ = qT.shape
    S = nkb * BLK
    nb = min(ATTN_BATCH, B)
    assert B % nb == 0
    q_spec, k_spec, v_spec, o_spec = _attn_specs(nb, S, nkb)
    nc = 2 * nb
    return pl.pallas_call(
        functools.partial(_sb_kernel, nb=nb),
        grid=(B // nb, N_HEADS // 2, nkb),
        in_specs=[q_spec, k_spec, v_spec, pl.BlockSpec((BLK, BLK), lambda g, p, q: (0, 0))],
        out_specs=o_spec,
        out_shape=jax.ShapeDtypeStruct((B, width, S), BF16),
        scratch_shapes=[pltpu.VMEM((nc, PAIR, BLK), BF16),
                        pltpu.VMEM((nc, HEAD_DIM, BLK), F32),
                        pltpu.VMEM((nc, 1, BLK), F32)],
        compiler_params=pltpu.CompilerParams(
            dimension_semantics=("arbitrary", "arbitrary", "arbitrary"),
            vmem_limit_bytes=VMEM_LIMIT_BYTES),
        name="stickbreak",
    )(qT, k, vT, us)


def _bias_kernel(table_ref, bucket_ref, bias_ref):
    h = pl.program_id(0)
    for t in range(2):
        bk = bucket_ref[t]
        out = jnp.full(bk.shape, NEG_INF, F32)
        for i in range(REL_BUCKETS):
            out = jnp.where(bk == i, table_ref[i * N_HEADS + h] * LOG2E, out)
        bias_ref[0, t] = out


def _t5_bucket_np(dist):
    n = np.maximum(dist, 0)
    max_exact = REL_BUCKETS // 2
    nf = np.maximum(n, 1).astype(np.float64)
    large = max_exact + (np.log(nf / max_exact) / math.log(REL_MAX_DIST / max_exact)
                         * (REL_BUCKETS - max_exact)).astype(np.int32)
    large = np.minimum(large, REL_BUCKETS - 1)
    return np.where(n < max_exact, n, large).astype(np.int32)


def _bias_tiles(rel_table):
    s = np.arange(BLK)[:, None]
    t = np.arange(BLK)[None, :]
    own = np.where(t - s >= 0, _t5_bucket_np(t - s), -1)
    buckets = np.stack([own, _t5_bucket_np(t - s + BLK)]).astype(np.int32)
    return pl.pallas_call(
        _bias_kernel,
        grid_spec=pltpu.PrefetchScalarGridSpec(
            num_scalar_prefetch=1,
            grid=(N_HEADS,),
            in_specs=[pl.BlockSpec((2, BLK, BLK), lambda h, tab: (0, 0, 0))],
            out_specs=pl.BlockSpec((1, 2, BLK, BLK), lambda h, tab: (h, 0, 0, 0)),
        ),
        out_shape=jax.ShapeDtypeStruct((N_HEADS, 2, BLK, BLK), F32),
        name="moba_bias",
    )(rel_table.reshape(-1), jnp.asarray(buckets))


def _moba_kernel(far_ref, qT_ref, k_ref, vT_ref, bias_ref, oT_ref,
                 q2_ref, kmean_ref, knorm_ref, sel_ref, acc_ref, m_ref, *, nb, nkb, gate_scale):
    pair = pl.program_id(1)
    own = pl.program_id(2)
    chains = _chains(nb)
    _stage_pair_queries(qT_ref, q2_ref, nb)

    @pl.when(own == 0)
    def _():
        for b in range(nb):
            for n in range(nkb):
                kb = k_ref[b, n * BLK:(n + 1) * BLK, :].astype(F32)
                km = jnp.mean(kb, axis=0, keepdims=True)
                for part in range(3):
                    piece = km.astype(BF16)
                    kmean_ref[b, part * nkb + n:part * nkb + n + 1, :] = piece
                    km = km - piece.astype(F32)
        lane = lax.broadcasted_iota(jnp.int32, (PAIR, PAIR), 0) // HEAD_DIM
        same_head = (lane == lax.broadcasted_iota(jnp.int32, (PAIR, PAIR), 1) // HEAD_DIM).astype(BF16)
        for b in range(nb):
            kf = k_ref[b].astype(F32)
            norm2 = jnp.max(jnp.dot((kf * kf).astype(BF16), same_head, preferred_element_type=F32),
                            axis=0, keepdims=True)
            for hh in range(2):
                knorm_ref[2 * b + hh] = jnp.broadcast_to(
                    jnp.sqrt(norm2[:, hh * HEAD_DIM:hh * HEAD_DIM + 1]), (1, BLK))

    n_ok = jnp.minimum(own, MOBA_TOPK)
    for c, (b, hh) in enumerate(chains):
        parts = jnp.dot(kmean_ref[b], q2_ref[c], preferred_element_type=F32)
        gate = (parts[:nkb] + parts[nkb:2 * nkb] + parts[2 * nkb:]) * gate_scale
        blk = lax.broadcasted_iota(jnp.int32, gate.shape, 0)
        g = jnp.where(blk < own, gate, NEG_INF)
        sel = jnp.zeros(gate.shape, F32)
        for j in range(min(MOBA_TOPK, nkb)):
            top = jnp.max(g, axis=0, keepdims=True)
            idx = jnp.min(jnp.where(g == top, blk, nkb), axis=0, keepdims=True)
            pick = blk == idx
            sel = jnp.where(pick & (j < n_ok), 1.0, sel)
            g = jnp.where(pick, -jnp.inf, g)
        sel_ref[c] = sel

    m_ref[...] = jnp.full(m_ref.shape, NEG_INF, F32)
    acc_ref[...] = jnp.zeros_like(acc_ref)
    far = [far_ref[2 * pair + hh] * LOG2E for hh in range(2)]
    ones = jnp.ones((SUM_ROWS, BLK), BF16)

    def attend(blocks):
        zcs = [[jnp.dot(k_ref[b, pl.ds(pl.multiple_of(n * BLK, BLK), BLK), :], q2_ref[c],
                        preferred_element_type=F32) for n, kind in blocks]
               for c, (b, hh) in enumerate(chains)]
        pss, alphas = [], []
        for c, (b, hh) in enumerate(chains):
            m_old = m_ref[c]
            m_new = m_old
            terms = []
            for (n, kind), z in zip(blocks, zcs[c]):
                if kind == "own":
                    s, shift, chosen = z + bias_ref[hh, 0], 0.0, None
                elif kind == "prev":
                    s, shift, chosen = z + bias_ref[hh, 1], 0.0, sel_ref[c, pl.ds(n, 1), :] > 0.0
                else:
                    s, shift, chosen = z, far[hh], sel_ref[c, pl.ds(n, 1), :] > 0.0
                m_blk = jnp.max(s, axis=0, keepdims=True) + shift
                if chosen is not None:
                    m_blk = jnp.where(chosen, m_blk, NEG_INF)
                m_new = jnp.maximum(m_new, m_blk)
                terms.append((s, shift, chosen))
            alphas.append(jnp.exp2(m_old - m_new))
            ps = []
            for s, shift, chosen in terms:
                sub = m_new - shift
                if chosen is not None:
                    sub = jnp.where(chosen, sub, -NEG_INF)
                ps.append(jnp.exp2(s - sub).astype(BF16))
            m_ref[c] = m_new
            pss.append(ps)
        for c, (b, hh) in enumerate(chains):
            acc = alphas[c] * acc_ref[c]
            for (n, kind), p in zip(blocks, pss[c]):
                v1 = jnp.concatenate([vT_ref[b, n, hh * HEAD_DIM:(hh + 1) * HEAD_DIM, :], ones], axis=0)
                acc = acc + jnp.dot(v1, p, preferred_element_type=F32)
            acc_ref[c] = acc

    def attend_fixed(ns):
        zcs = [[jnp.dot(k_ref[b, pl.ds(pl.multiple_of(n * BLK, BLK), BLK), :], q2_ref[c],
                        preferred_element_type=F32) for n in ns]
               for c, (b, hh) in enumerate(chains)]
        pss = []
        for c, (b, hh) in enumerate(chains):
            ref = m_ref[c] - far[hh]
            ps = []
            for n, z in zip(ns, zcs[c]):
                chosen = sel_ref[c, pl.ds(n, 1), :] > 0.0
                ps.append(jnp.exp2(z - jnp.where(chosen, ref, -NEG_INF)).astype(BF16))
            pss.append(ps)
        for c, (b, hh) in enumerate(chains):
            acc = acc_ref[c]
            for n, p in zip(ns, pss[c]):
                v1 = jnp.concatenate([vT_ref[b, n, hh * HEAD_DIM:(hh + 1) * HEAD_DIM, :], ones], axis=0)
                acc = acc + jnp.dot(v1, p, preferred_element_type=F32)
            acc_ref[c] = acc

    attend([(own, "own"), (jnp.maximum(own - 1, 0), "prev")])
    n_far = jnp.maximum(own - 1, 0)
    rem = n_far % MOBA_GROUP

    worst = jnp.zeros((1, BLK), F32)
    for c, (b, hh) in enumerate(chains):
        q = q2_ref[c].astype(F32)
        qnorm = jnp.sqrt(jnp.sum(q * q, axis=0, keepdims=True))
        bound = knorm_ref[c] * qnorm * NORM_SLACK + far[hh]
        worst = jnp.maximum(worst, jnp.where(bound <= m_ref[c] + FIXED_MAX_MARGIN, 0.0, 1.0))
    fixed_ok = jnp.max(worst) == 0.0

    def far_loops(step):
        def single(i, carry):
            step([i])
            return carry

        def group(j, carry):
            step([rem + MOBA_GROUP * j + u for u in range(MOBA_GROUP)])
            return carry

        lax.fori_loop(0, rem, single, 0)
        lax.fori_loop(0, n_far // MOBA_GROUP, group, 0)

    @pl.when(fixed_ok)
    def _():
        far_loops(attend_fixed)

    @pl.when(jnp.logical_not(fixed_ok))
    def _():
        far_loops(lambda ns: attend([(n, "far") for n in ns]))
    for c, (b, hh) in enumerate(chains):
        acc = acc_ref[c]
        oT_ref[b, hh * HEAD_DIM:(hh + 1) * HEAD_DIM, :] = (
            acc[:HEAD_DIM] / acc[HEAD_DIM:HEAD_DIM + 1]).astype(BF16)


def _moba(qT, k, vT, bias, far, *, gate_scale):
    B, nkb, width, _ = qT.shape
    S = nkb * BLK
    nb = min(ATTN_BATCH, B)
    assert B % nb == 0
    q_spec, k_spec, v_spec, o_spec = _attn_specs(nb, S, nkb, extra_args=1)
    nc = 2 * nb
    kern = functools.partial(_moba_kernel, nb=nb, nkb=nkb, gate_scale=gate_scale)
    return pl.pallas_call(
        kern,
        grid_spec=pltpu.PrefetchScalarGridSpec(
            num_scalar_prefetch=1,
            grid=(B // nb, N_HEADS // 2, nkb),
            in_specs=[q_spec, k_spec, v_spec,
                      pl.BlockSpec((2, 2, BLK, BLK), lambda g, p, q, far: (p, 0, 0, 0))],
            out_specs=o_spec,
            scratch_shapes=[pltpu.VMEM((nc, PAIR, BLK), BF16),
                            pltpu.VMEM((nb, 3 * nkb, PAIR), BF16),
                            pltpu.VMEM((nc, 1, BLK), F32),
                            pltpu.VMEM((nc, nkb, BLK), F32),
                            pltpu.VMEM((nc, HEAD_DIM + SUM_ROWS, BLK), F32),
                            pltpu.VMEM((nc, 1, BLK), F32)],
        ),
        out_shape=jax.ShapeDtypeStruct((B, width, S), BF16),
        compiler_params=pltpu.CompilerParams(
            dimension_semantics=("arbitrary", "arbitrary", "arbitrary"),
            vmem_limit_bytes=VMEM_LIMIT_BYTES),
        name="moba",
    )(far, qT, k, vT, bias)


def _out_kernel(oT_ref, gT_ref, x_ref, woT_ref, lng_ref, lnb_ref, h_ref, yT_ref):
    g = gT_ref[0].astype(F32)
    og = (oT_ref[0].astype(F32) * (g * (1.0 / (1.0 + jnp.exp(-g))))).astype(BF16)
    yT_ref[...] = jnp.dot(woT_ref[...], og, preferred_element_type=F32)
    r = DEEPNORM_ALPHA * x_ref[0] + yT_ref[...].T
    mu = jnp.mean(r, axis=-1, keepdims=True)
    d = r - mu
    var = jnp.mean(d * d, axis=-1, keepdims=True)
    h_ref[0] = d * lax.rsqrt(var + LN_EPS) * lng_ref[...] + lnb_ref[...]


def _out_proj_ln(oT, gT, x, woT, lng, lnb):
    B, S, D = x.shape
    width = oT.shape[1]
    rows = min(PROJ_ROWS, S)
    return pl.pallas_call(
        _out_kernel,
        grid=(B, S // rows),
        in_specs=[
            pl.BlockSpec((1, width, rows), lambda b, s: (b, 0, s)),
            pl.BlockSpec((1, width, rows), lambda b, s: (b, 0, s)),
            pl.BlockSpec((1, rows, D), lambda b, s: (b, s, 0)),
            pl.BlockSpec((D, width), lambda b, s: (0, 0)),
            pl.BlockSpec((1, D), lambda b, s: (0, 0)),
            pl.BlockSpec((1, D), lambda b, s: (0, 0)),
        ],
        out_specs=pl.BlockSpec((1, rows, D), lambda b, s: (b, s, 0)),
        out_shape=jax.ShapeDtypeStruct((B, S, D), F32),
        scratch_shapes=[pltpu.VMEM((D, rows), F32)],
        compiler_params=pltpu.CompilerParams(
            dimension_semantics=("arbitrary", "arbitrary"), vmem_limit_bytes=VMEM_LIMIT_BYTES),
        name="outproj_ln",
    )(oT, gT, x, woT, lng, lnb)


def _suffix_sum_matrix():
    return jnp.asarray(np.triu(np.ones((BLK, BLK), np.float32), k=1), BF16)


def kernel(x, w_in, w_out, ln_g, ln_b, rel_table):
    width = N_HEADS * HEAD_DIM
    scale = HEAD_DIM ** -0.5 * LOG2E
    us = _suffix_sum_matrix()
    bias = _bias_tiles(rel_table)
    far = rel_table[REL_BUCKETS - 1]
    h = x
    for layer in range(DEPTH):
        woT = w_out[layer].T.astype(BF16)
        k, qT, vT, gT = _project(h, w_in[layer], width=width, scale=scale)
        if layer % 2 == 0:
            oT = _stick_breaking(qT, k, vT, us)
        else:
            oT = _moba(qT, k, vT, bias, far, gate_scale=1.0 / scale)
        h = _out_proj_ln(oT, gT, h, woT, ln_g[layer][None, :], ln_b[layer][None, :])
    return h
```

```python
import functools
import math

import jax
import jax.numpy as jnp
import numpy as np
from jax import lax
from jax.experimental import pallas as pl
from jax.experimental.pallas import tpu as pltpu

N_HEADS = 16
HEAD_DIM = 64
DEPTH = 2
MOBA_BLOCK = 256
MOBA_TOPK = 3
REL_BUCKETS = 32
REL_MAX_DIST = 128
LN_EPS = 1e-5
DEEPNORM_ALPHA = (2.0 * DEPTH) ** 0.25
NEG_INF = -1e30
LOG2E = math.log2(math.e)
EXP2_CLAMP = 100.0

BLK = 256
PROJ_ROWS = 512
ATTN_BATCH = 4
SB_GROUP = 4
SB_TAIL_GROUPS = (2, 1)
SB_LAG = (2, 4)
MOBA_GROUP, MOBA_TAILS = 2, (1,)
MOBA_FIXED_GROUP, MOBA_FIXED_TAILS = 4, (2, 1)
SUM_ROWS = 16
FIXED_MAX_MARGIN = 60.0
NORM_SLACK = 1.0 + 2.0 ** -6
VMEM_LIMIT_BYTES = 56 * 1024 * 1024

F32 = jnp.float32
BF16 = jnp.bfloat16


def _proj_kernel(x_ref, w_ref, k_ref, qT_ref, vT_ref, gT_ref, q_scr, v_scr, g_scr, *, width, scale):
    xb = x_ref[0].astype(BF16)
    n_blk = xb.shape[0] // BLK

    def part(i):
        return jnp.dot(xb, w_ref[0, :, i * width:(i + 1) * width].astype(BF16),
                       preferred_element_type=F32)

    k_ref[0] = part(1).astype(BF16)
    q_scr[...] = part(0) * scale
    v_scr[...] = part(2)
    g_scr[...] = part(3)
    qT = q_scr[...].T.astype(BF16)
    vT = v_scr[...].T.astype(BF16)
    for j in range(n_blk):
        qT_ref[0, j] = qT[:, j * BLK:(j + 1) * BLK]
        vT_ref[0, j] = vT[:, j * BLK:(j + 1) * BLK]
    gT_ref[0] = g_scr[...].T.astype(BF16)


def _project(x, w_in, layer, *, width, scale):
    B, S, D = x.shape
    rows = min(PROJ_ROWS, S)
    n_blk = rows // BLK
    nkb = S // BLK
    kern = functools.partial(_proj_kernel, width=width, scale=scale)
    return pl.pallas_call(
        kern,
        grid=(B, S // rows),
        in_specs=[
            pl.BlockSpec((1, rows, D), lambda b, s: (b, s, 0)),
            pl.BlockSpec((1, D, 4 * width), lambda b, s: (layer, 0, 0), pipeline_mode=pl.Buffered(1)),
        ],
        out_specs=[
            pl.BlockSpec((1, rows, width), lambda b, s: (b, s, 0)),
            pl.BlockSpec((1, n_blk, width, BLK), lambda b, s: (b, s, 0, 0)),
            pl.BlockSpec((1, n_blk, width, BLK), lambda b, s: (b, s, 0, 0)),
            pl.BlockSpec((1, width, rows), lambda b, s: (b, 0, s)),
        ],
        out_shape=[
            jax.ShapeDtypeStruct((B, S, width), BF16),
            jax.ShapeDtypeStruct((B, nkb, width, BLK), BF16),
            jax.ShapeDtypeStruct((B, nkb, width, BLK), BF16),
            jax.ShapeDtypeStruct((B, width, S), BF16),
        ],
        scratch_shapes=[pltpu.VMEM((rows, width), F32)] * 3,
        compiler_params=pltpu.CompilerParams(
            dimension_semantics=("arbitrary", "arbitrary"), vmem_limit_bytes=VMEM_LIMIT_BYTES),
        name="proj",
    )(x, w_in)


PAIR = 2 * HEAD_DIM


def _chains(nb):
    return [(b, hh) for b in range(nb) for hh in range(2)]


def _stage_pair_queries(qT_ref, q2_ref, nb):
    zeros = jnp.zeros((HEAD_DIM, BLK), BF16)
    for c, (b, hh) in enumerate(_chains(nb)):
        q = qT_ref[b, 0, hh * HEAD_DIM:(hh + 1) * HEAD_DIM, :]
        q2_ref[c] = jnp.concatenate([q, zeros] if hh == 0 else [zeros, q], axis=0)


def _attn_specs(nb, S, nkb, extra_args=0):
    def im(f):
        if extra_args:
            return lambda g, p, q, *_: f(g, p, q)
        return f
    q_spec = pl.BlockSpec((nb, 1, PAIR, BLK), im(lambda g, p, q: (g, q, p, 0)))
    k_spec = pl.BlockSpec((nb, S, PAIR), im(lambda g, p, q: (g, 0, p)))
    v_spec = pl.BlockSpec((nb, nkb, PAIR, BLK), im(lambda g, p, q: (g, 0, p, 0)))
    o_spec = pl.BlockSpec((nb, PAIR, BLK), im(lambda g, p, q: (g, p, q)))
    return q_spec, k_spec, v_spec, o_spec


def _sb_kernel(qT_ref, k_ref, vT_ref, us_ref, oT_ref, q2_ref, acc_ref, r_ref, *, nb):
    qi = pl.program_id(2)
    chains = _chains(nb)
    _stage_pair_queries(qT_ref, q2_ref, nb)
    acc_ref[...] = jnp.zeros_like(acc_ref)
    r_ref[...] = jnp.zeros_like(r_ref)

    def process(kbs, masked):
        if masked:
            row = lax.broadcasted_iota(jnp.int32, (BLK, BLK), 0)
            col = lax.broadcasted_iota(jnp.int32, (BLK, BLK), 1)
            past = row < col
        tiles = [(kb, c, b, hh) for kb in kbs for c, (b, hh) in enumerate(chains)]
        n_tiles = len(tiles)
        zs, ys, ps, css, pvs = {}, {}, {}, {}, {}
        lag_p, lag_w = SB_LAG
        for step in range(n_tiles + lag_w):
            if step < n_tiles:
                kb, c, b, hh = tiles[step]
                zs[step] = jnp.dot(k_ref[b, pl.ds(pl.multiple_of(kb * BLK, BLK), BLK), :], q2_ref[c],
                                   preferred_element_type=F32)
            i = step - lag_p
            if 0 <= i < n_tiles:
                z = zs.pop(i)
                p = jnp.maximum(jnp.log(1.0 + jnp.exp2(jnp.minimum(z, EXP2_CLAMP))) * LOG2E, z)
                if masked:
                    p = jnp.where(past, p, 0.0)
                ys[i] = z - p
                ps[i] = p.astype(BF16)
                css[i] = jnp.dot(us_ref[...], ps[i], preferred_element_type=F32)
            j = step - lag_w
            if 0 <= j < n_tiles:
                kb, c, b, hh = tiles[j]
                w = jnp.exp2(ys.pop(j) - css[j])
                if masked:
                    w = jnp.where(past, w, 0.0)
                pvs[j] = jnp.dot(vT_ref[b, kb, hh * HEAD_DIM:(hh + 1) * HEAD_DIM, :], w.astype(BF16),
                                 preferred_element_type=F32)
        for c in range(len(chains)):
            r = r_ref[c]
            acc = acc_ref[c]
            for t, (kb, tc, b, hh) in enumerate(tiles):
                if tc == c:
                    acc = acc + pvs[t] * jnp.exp2(-r)
                    r = r + css[t][0:1, :] + ps[t][0:1, :].astype(F32)
            acc_ref[c] = acc
            r_ref[c] = r

    process([qi], True)
    n_big = qi // SB_GROUP
    left = qi - n_big * SB_GROUP

    def group(j, carry):
        first = qi - 1 - SB_GROUP * j
        process([first - u for u in range(SB_GROUP)], False)
        return carry

    lax.fori_loop(0, n_big, group, 0)
    for size in SB_TAIL_GROUPS:
        @pl.when(left // size % 2 == 1)
        def _(size=size):
            first = left % (2 * size) - 1
            process([first - u for u in range(size)], False)
    for c, (b, hh) in enumerate(chains):
        oT_ref[b, hh * HEAD_DIM:(hh + 1) * HEAD_DIM, :] = acc_ref[c].astype(BF16)


def _stick_breaking(qT, k, vT, us):
    B, nkb, width, _ = qT.shape
    S = nkb * BLK
    nb = min(ATTN_BATCH, B)
    assert B % nb == 0
    q_spec, k_spec, v_spec, o_spec = _attn_specs(nb, S, nkb)
    nc = 2 * nb
    return pl.pallas_call(
        functools.partial(_sb_kernel, nb=nb),
        grid=(B // nb, N_HEADS // 2, nkb),
        in_specs=[q_spec, k_spec, v_spec, pl.BlockSpec((BLK, BLK), lambda g, p, q: (0, 0))],
        out_specs=o_spec,
        out_shape=jax.ShapeDtypeStruct((B, width, S), BF16),
        scratch_shapes=[pltpu.VMEM((nc, PAIR, BLK), BF16),
                        pltpu.VMEM((nc, HEAD_DIM, BLK), F32),
                        pltpu.VMEM((nc, 1, BLK), F32)],
        compiler_params=pltpu.CompilerParams(
            dimension_semantics=("arbitrary", "arbitrary", "arbitrary"),
            vmem_limit_bytes=VMEM_LIMIT_BYTES),
        name="stickbreak",
    )(qT, k, vT, us)


def _bias_kernel(table_ref, bucket_ref, bias_ref):
    h = pl.program_id(0)
    bk = bucket_ref[...]
    by_dist = jnp.full(bk.shape, NEG_INF, F32)
    for i in range(REL_BUCKETS):
        by_dist = jnp.where(bk == i, table_ref[i * N_HEADS + h] * LOG2E, by_dist)
    rows = jnp.broadcast_to(by_dist, (BLK, 4 * BLK))
    skew = pltpu.roll(rows, 0, 1, stride=1, stride_axis=0)
    bias_ref[0, 0] = skew[:, BLK:2 * BLK]
    bias_ref[0, 1] = skew[:, 2 * BLK:3 * BLK]


def _t5_bucket_np(dist):
    n = np.maximum(dist, 0)
    max_exact = REL_BUCKETS // 2
    nf = np.maximum(n, 1).astype(np.float64)
    large = max_exact + (np.log(nf / max_exact) / math.log(REL_MAX_DIST / max_exact)
                         * (REL_BUCKETS - max_exact)).astype(np.int32)
    large = np.minimum(large, REL_BUCKETS - 1)
    return np.where(n < max_exact, n, large).astype(np.int32)


def _bias_tiles(rel_table):
    dist = np.arange(4 * BLK) - BLK
    buckets = np.where(dist >= 0, _t5_bucket_np(dist), -1).astype(np.int32)[None, :]
    return pl.pallas_call(
        _bias_kernel,
        grid_spec=pltpu.PrefetchScalarGridSpec(
            num_scalar_prefetch=1,
            grid=(N_HEADS,),
            in_specs=[pl.BlockSpec((1, 4 * BLK), lambda h, tab: (0, 0))],
            out_specs=pl.BlockSpec((1, 2, BLK, BLK), lambda h, tab: (h, 0, 0, 0)),
        ),
        out_shape=jax.ShapeDtypeStruct((N_HEADS, 2, BLK, BLK), F32),
        name="moba_bias",
    )(rel_table.reshape(-1), jnp.asarray(buckets))


def _moba_kernel(far_ref, qT_ref, k_ref, vT_ref, bias_ref, oT_ref,
                 q2_ref, kmean_ref, knorm_ref, sel_ref, acc_ref, m_ref, *, nb, nkb, gate_scale):
    pair = pl.program_id(1)
    own = pl.program_id(2)
    chains = _chains(nb)
    _stage_pair_queries(qT_ref, q2_ref, nb)

    @pl.when(own == 0)
    def _():
        for b in range(nb):
            for n in range(nkb):
                kb = k_ref[b, n * BLK:(n + 1) * BLK, :].astype(F32)
                km = jnp.mean(kb, axis=0, keepdims=True)
                for part in range(3):
                    piece = km.astype(BF16)
                    kmean_ref[b, part * nkb + n:part * nkb + n + 1, :] = piece
                    km = km - piece.astype(F32)
        lane = lax.broadcasted_iota(jnp.int32, (PAIR, PAIR), 0) // HEAD_DIM
        same_head = (lane == lax.broadcasted_iota(jnp.int32, (PAIR, PAIR), 1) // HEAD_DIM).astype(BF16)
        for b in range(nb):
            kf = k_ref[b].astype(F32)
            norm2 = jnp.max(jnp.dot((kf * kf).astype(BF16), same_head, preferred_element_type=F32),
                            axis=0, keepdims=True)
            for hh in range(2):
                knorm_ref[2 * b + hh] = jnp.broadcast_to(
                    jnp.sqrt(norm2[:, hh * HEAD_DIM:hh * HEAD_DIM + 1]), (1, BLK))

    n_ok = jnp.minimum(own, MOBA_TOPK)
    for c, (b, hh) in enumerate(chains):
        parts = jnp.dot(kmean_ref[b], q2_ref[c], preferred_element_type=F32)
        gate = (parts[:nkb] + parts[nkb:2 * nkb] + parts[2 * nkb:]) * gate_scale
        blk = lax.broadcasted_iota(jnp.int32, gate.shape, 0)
        g = jnp.where(blk < own, gate, NEG_INF)
        sel = jnp.zeros(gate.shape, F32)
        for j in range(min(MOBA_TOPK, nkb)):
            top = jnp.max(g, axis=0, keepdims=True)
            idx = jnp.min(jnp.where(g == top, blk, nkb), axis=0, keepdims=True)
            pick = blk == idx
            sel = jnp.where(pick & (j < n_ok), 1.0, sel)
            g = jnp.where(pick, -jnp.inf, g)
        sel_ref[c] = sel

    m_ref[...] = jnp.full(m_ref.shape, NEG_INF, F32)
    acc_ref[...] = jnp.zeros_like(acc_ref)
    far = [far_ref[2 * pair + hh] * LOG2E for hh in range(2)]
    ones = jnp.ones((SUM_ROWS, BLK), BF16)

    def attend(blocks):
        zcs = [[jnp.dot(k_ref[b, pl.ds(pl.multiple_of(n * BLK, BLK), BLK), :], q2_ref[c],
                        preferred_element_type=F32) for n, kind in blocks]
               for c, (b, hh) in enumerate(chains)]
        pss, alphas = [], []
        for c, (b, hh) in enumerate(chains):
            m_old = m_ref[c]
            m_new = m_old
            terms = []
            for (n, kind), z in zip(blocks, zcs[c]):
                if kind == "own":
                    s, shift, chosen = z + bias_ref[hh, 0], 0.0, None
                elif kind == "prev":
                    s, shift, chosen = z + bias_ref[hh, 1], 0.0, sel_ref[c, pl.ds(n, 1), :] > 0.0
                else:
                    s, shift, chosen = z, far[hh], sel_ref[c, pl.ds(n, 1), :] > 0.0
                m_blk = jnp.max(s, axis=0, keepdims=True) + shift
                if chosen is not None:
                    m_blk = jnp.where(chosen, m_blk, NEG_INF)
                m_new = jnp.maximum(m_new, m_blk)
                terms.append((s, shift, chosen))
            alphas.append(jnp.exp2(m_old - m_new))
            ps = []
            for s, shift, chosen in terms:
                sub = m_new - shift
                if chosen is not None:
                    sub = jnp.where(chosen, sub, -NEG_INF)
                ps.append(jnp.exp2(s - sub).astype(BF16))
            m_ref[c] = m_new
            pss.append(ps)
        for c, (b, hh) in enumerate(chains):
            acc = alphas[c] * acc_ref[c]
            for (n, kind), p in zip(blocks, pss[c]):
                v1 = jnp.concatenate([vT_ref[b, n, hh * HEAD_DIM:(hh + 1) * HEAD_DIM, :], ones], axis=0)
                acc = acc + jnp.dot(v1, p, preferred_element_type=F32)
            acc_ref[c] = acc

    def attend_fixed(ns):
        zcs = [[jnp.dot(k_ref[b, pl.ds(pl.multiple_of(n * BLK, BLK), BLK), :], q2_ref[c],
                        preferred_element_type=F32) for n in ns]
               for c, (b, hh) in enumerate(chains)]
        pss = []
        for c, (b, hh) in enumerate(chains):
            ref = m_ref[c] - far[hh]
            ps = []
            for n, z in zip(ns, zcs[c]):
                chosen = sel_ref[c, pl.ds(n, 1), :] > 0.0
                ps.append(jnp.exp2(z - jnp.where(chosen, ref, -NEG_INF)).astype(BF16))
            pss.append(ps)
        for c, (b, hh) in enumerate(chains):
            acc = acc_ref[c]
            for n, p in zip(ns, pss[c]):
                v1 = jnp.concatenate([vT_ref[b, n, hh * HEAD_DIM:(hh + 1) * HEAD_DIM, :], ones], axis=0)
                acc = acc + jnp.dot(v1, p, preferred_element_type=F32)
            acc_ref[c] = acc

    attend([(own, "own"), (jnp.maximum(own - 1, 0), "prev")])
    n_far = jnp.maximum(own - 1, 0)

    worst = jnp.zeros((1, BLK), F32)
    for c, (b, hh) in enumerate(chains):
        q = q2_ref[c].astype(F32)
        qnorm = jnp.sqrt(jnp.sum(q * q, axis=0, keepdims=True))
        bound = knorm_ref[c] * qnorm * NORM_SLACK + far[hh]
        worst = jnp.maximum(worst, jnp.where(bound <= m_ref[c] + FIXED_MAX_MARGIN, 0.0, 1.0))
    fixed_ok = jnp.max(worst) == 0.0

    def far_loops(step, group, tail_groups):
        n_big = n_far // group
        left = n_far - n_big * group

        def body(j, carry):
            step([group * j + u for u in range(group)])
            return carry

        lax.fori_loop(0, n_big, body, 0)
        for size in tail_groups:
            @pl.when(left // size % 2 == 1)
            def _(size=size):
                first = n_far - left % (2 * size)
                step([first + u for u in range(size)])

    @pl.when(fixed_ok)
    def _():
        far_loops(attend_fixed, MOBA_FIXED_GROUP, MOBA_FIXED_TAILS)

    @pl.when(jnp.logical_not(fixed_ok))
    def _():
        far_loops(lambda ns: attend([(n, "far") for n in ns]), MOBA_GROUP, MOBA_TAILS)

    for c, (b, hh) in enumerate(chains):
        acc = acc_ref[c]
        oT_ref[b, hh * HEAD_DIM:(hh + 1) * HEAD_DIM, :] = (
            acc[:HEAD_DIM] / acc[HEAD_DIM:HEAD_DIM + 1]).astype(BF16)


def _moba(qT, k, vT, bias, far, *, gate_scale):
    B, nkb, width, _ = qT.shape
    S = nkb * BLK
    nb = min(ATTN_BATCH, B)
    assert B % nb == 0
    q_spec, k_spec, v_spec, o_spec = _attn_specs(nb, S, nkb, extra_args=1)
    nc = 2 * nb
    kern = functools.partial(_moba_kernel, nb=nb, nkb=nkb, gate_scale=gate_scale)
    return pl.pallas_call(
        kern,
        grid_spec=pltpu.PrefetchScalarGridSpec(
            num_scalar_prefetch=1,
            grid=(B // nb, N_HEADS // 2, nkb),
            in_specs=[q_spec, k_spec, v_spec,
                      pl.BlockSpec((2, 2, BLK, BLK), lambda g, p, q, far: (p, 0, 0, 0))],
            out_specs=o_spec,
            scratch_shapes=[pltpu.VMEM((nc, PAIR, BLK), BF16),
                            pltpu.VMEM((nb, 3 * nkb, PAIR), BF16),
                            pltpu.VMEM((nc, 1, BLK), F32),
                            pltpu.VMEM((nc, nkb, BLK), F32),
                            pltpu.VMEM((nc, HEAD_DIM + SUM_ROWS, BLK), F32),
                            pltpu.VMEM((nc, 1, BLK), F32)],
        ),
        out_shape=jax.ShapeDtypeStruct((B, width, S), BF16),
        compiler_params=pltpu.CompilerParams(
            dimension_semantics=("arbitrary", "arbitrary", "arbitrary"),
            vmem_limit_bytes=VMEM_LIMIT_BYTES),
        name="moba",
    )(far, qT, k, vT, bias)


def _out_kernel(oT_ref, gT_ref, x_ref, woT_ref, lng_ref, lnb_ref, h_ref, yT_ref):
    g = gT_ref[0].astype(F32)
    og = (oT_ref[0].astype(F32) * (g * (1.0 / (1.0 + jnp.exp(-g))))).astype(BF16)
    yT_ref[...] = jnp.dot(woT_ref[...], og, preferred_element_type=F32)
    r = DEEPNORM_ALPHA * x_ref[0] + yT_ref[...].T
    mu = jnp.mean(r, axis=-1, keepdims=True)
    d = r - mu
    var = jnp.mean(d * d, axis=-1, keepdims=True)
    h_ref[0] = d * lax.rsqrt(var + LN_EPS) * lng_ref[...] + lnb_ref[...]


def _out_proj_ln(oT, gT, x, woT, lng, lnb):
    B, S, D = x.shape
    width = oT.shape[1]
    rows = min(PROJ_ROWS, S)
    return pl.pallas_call(
        _out_kernel,
        grid=(B, S // rows),
        in_specs=[
            pl.BlockSpec((1, width, rows), lambda b, s: (b, 0, s)),
            pl.BlockSpec((1, width, rows), lambda b, s: (b, 0, s)),
            pl.BlockSpec((1, rows, D), lambda b, s: (b, s, 0)),
            pl.BlockSpec((D, width), lambda b, s: (0, 0)),
            pl.BlockSpec((1, D), lambda b, s: (0, 0)),
            pl.BlockSpec((1, D), lambda b, s: (0, 0)),
        ],
        out_specs=pl.BlockSpec((1, rows, D), lambda b, s: (b, s, 0)),
        out_shape=jax.ShapeDtypeStruct((B, S, D), F32),
        scratch_shapes=[pltpu.VMEM((D, rows), F32)],
        compiler_params=pltpu.CompilerParams(
            dimension_semantics=("arbitrary", "arbitrary"), vmem_limit_bytes=VMEM_LIMIT_BYTES),
        name="outproj_ln",
    )(oT, gT, x, woT, lng, lnb)


def _suffix_sum_matrix():
    return jnp.asarray(np.triu(np.ones((BLK, BLK), np.float32), k=1), BF16)


def kernel(x, w_in, w_out, ln_g, ln_b, rel_table):
    width = N_HEADS * HEAD_DIM
    scale = HEAD_DIM ** -0.5 * LOG2E
    us = _suffix_sum_matrix()
    bias = _bias_tiles(rel_table)
    far = rel_table[REL_BUCKETS - 1]
    h = x
    for layer in range(DEPTH):
        woT = w_out[layer].T.astype(BF16)
        k, qT, vT, gT = _project(h, w_in, layer, width=width, scale=scale)
        if layer % 2 == 0:
            oT = _stick_breaking(qT, k, vT, us)
        else:
            oT = _moba(qT, k, vT, bias, far, gate_scale=1.0 / scale)
        h = _out_proj_ln(oT, gT, h, woT, ln_g[layer][None, :], ln_b[layer][None, :])
    return h
```

```python
import functools
import math

import jax
import jax.numpy as jnp
import numpy as np
from jax import lax
from jax.experimental import pallas as pl
from jax.experimental.pallas import tpu as pltpu

N_HEADS = 16
HEAD_DIM = 64
DEPTH = 2
MOBA_BLOCK = 256
MOBA_TOPK = 3
REL_BUCKETS = 32
REL_MAX_DIST = 128
LN_EPS = 1e-5
DEEPNORM_ALPHA = (2.0 * DEPTH) ** 0.25
NEG_INF = -1e30
LOG2E = math.log2(math.e)
EXP2_CLAMP = 100.0

BLK = 256
PROJ_ROWS = 512
ATTN_BATCH = 4
SB_GROUP = 4
SB_TAIL_GROUPS = (2, 1)
SB_LAG = (2, 4)
MOBA_GROUP, MOBA_TAILS = 2, (1,)
MOBA_FIXED_GROUP, MOBA_FIXED_TAILS = 4, (2, 1)
SUM_ROWS = 16
FIXED_REF_MIN_SUM = 2.0 ** -52
NORM_SLACK = 1.0 + 2.0 ** -6
VMEM_LIMIT_BYTES = 56 * 1024 * 1024

F32 = jnp.float32
BF16 = jnp.bfloat16


def _proj_kernel(x_ref, w_ref, k_ref, qT_ref, vT_ref, gT_ref, q_scr, v_scr, g_scr, *, width, scale):
    xb = x_ref[0].astype(BF16)
    n_blk = xb.shape[0] // BLK

    def part(i):
        return jnp.dot(xb, w_ref[0, :, i * width:(i + 1) * width].astype(BF16),
                       preferred_element_type=F32)

    k_ref[0] = part(1).astype(BF16)
    q_scr[...] = part(0) * scale
    v_scr[...] = part(2)
    g_scr[...] = part(3)
    qT = q_scr[...].T.astype(BF16)
    vT = v_scr[...].T.astype(BF16)
    for j in range(n_blk):
        qT_ref[0, j] = qT[:, j * BLK:(j + 1) * BLK]
        vT_ref[0, j] = vT[:, j * BLK:(j + 1) * BLK]
    gT_ref[0] = g_scr[...].T.astype(BF16)


def _project(x, w_in, layer, *, width, scale):
    B, S, D = x.shape
    rows = min(PROJ_ROWS, S)
    n_blk = rows // BLK
    nkb = S // BLK
    kern = functools.partial(_proj_kernel, width=width, scale=scale)
    return pl.pallas_call(
        kern,
        grid=(B, S // rows),
        in_specs=[
            pl.BlockSpec((1, rows, D), lambda b, s: (b, s, 0)),
            pl.BlockSpec((1, D, 4 * width), lambda b, s: (layer, 0, 0), pipeline_mode=pl.Buffered(1)),
        ],
        out_specs=[
            pl.BlockSpec((1, rows, width), lambda b, s: (b, s, 0)),
            pl.BlockSpec((1, n_blk, width, BLK), lambda b, s: (b, s, 0, 0)),
            pl.BlockSpec((1, n_blk, width, BLK), lambda b, s: (b, s, 0, 0)),
            pl.BlockSpec((1, width, rows), lambda b, s: (b, 0, s)),
        ],
        out_shape=[
            jax.ShapeDtypeStruct((B, S, width), BF16),
            jax.ShapeDtypeStruct((B, nkb, width, BLK), BF16),
            jax.ShapeDtypeStruct((B, nkb, width, BLK), BF16),
            jax.ShapeDtypeStruct((B, width, S), BF16),
        ],
        scratch_shapes=[pltpu.VMEM((rows, width), F32)] * 3,
        compiler_params=pltpu.CompilerParams(
            dimension_semantics=("arbitrary", "arbitrary"), vmem_limit_bytes=VMEM_LIMIT_BYTES),
        name="proj",
    )(x, w_in)


PAIR = 2 * HEAD_DIM


def _chains(nb):
    return [(b, hh) for b in range(nb) for hh in range(2)]


def _stage_pair_queries(qT_ref, q2_ref, nb):
    zeros = jnp.zeros((HEAD_DIM, BLK), BF16)
    for c, (b, hh) in enumerate(_chains(nb)):
        q = qT_ref[b, 0, hh * HEAD_DIM:(hh + 1) * HEAD_DIM, :]
        q2_ref[c] = jnp.concatenate([q, zeros] if hh == 0 else [zeros, q], axis=0)


def _attn_specs(nb, S, nkb, extra_args=0):
    def im(f):
        if extra_args:
            return lambda g, p, q, *_: f(g, p, q)
        return f
    q_spec = pl.BlockSpec((nb, 1, PAIR, BLK), im(lambda g, p, q: (g, q, p, 0)))
    k_spec = pl.BlockSpec((nb, S, PAIR), im(lambda g, p, q: (g, 0, p)))
    v_spec = pl.BlockSpec((nb, nkb, PAIR, BLK), im(lambda g, p, q: (g, 0, p, 0)))
    o_spec = pl.BlockSpec((nb, PAIR, BLK), im(lambda g, p, q: (g, p, q)))
    return q_spec, k_spec, v_spec, o_spec


def _sb_kernel(qT_ref, k_ref, vT_ref, us_ref, oT_ref, q2_ref, acc_ref, r_ref, *, nb):
    qi = pl.program_id(2)
    chains = _chains(nb)
    _stage_pair_queries(qT_ref, q2_ref, nb)
    acc_ref[...] = jnp.zeros_like(acc_ref)
    r_ref[...] = jnp.zeros_like(r_ref)

    def process(kbs, masked):
        if masked:
            row = lax.broadcasted_iota(jnp.int32, (BLK, BLK), 0)
            col = lax.broadcasted_iota(jnp.int32, (BLK, BLK), 1)
            past = row < col
        tiles = [(kb, c, b, hh) for kb in kbs for c, (b, hh) in enumerate(chains)]
        n_tiles = len(tiles)
        zs, ys, ps, css, pvs = {}, {}, {}, {}, {}
        lag_p, lag_w = SB_LAG
        for step in range(n_tiles + lag_w):
            if step < n_tiles:
                kb, c, b, hh = tiles[step]
                zs[step] = jnp.dot(k_ref[b, pl.ds(pl.multiple_of(kb * BLK, BLK), BLK), :], q2_ref[c],
                                   preferred_element_type=F32)
            i = step - lag_p
            if 0 <= i < n_tiles:
                z = zs.pop(i)
                p = jnp.maximum(jnp.log(1.0 + jnp.exp2(jnp.minimum(z, EXP2_CLAMP))) * LOG2E, z)
                if masked:
                    p = jnp.where(past, p, 0.0)
                ys[i] = z - p
                ps[i] = p.astype(BF16)
                css[i] = jnp.dot(us_ref[...], ps[i], preferred_element_type=F32)
            j = step - lag_w
            if 0 <= j < n_tiles:
                kb, c, b, hh = tiles[j]
                w = jnp.exp2(ys.pop(j) - css[j])
                if masked:
                    w = jnp.where(past, w, 0.0)
                pvs[j] = jnp.dot(vT_ref[b, kb, hh * HEAD_DIM:(hh + 1) * HEAD_DIM, :], w.astype(BF16),
                                 preferred_element_type=F32)
        for c in range(len(chains)):
            r = r_ref[c]
            acc = acc_ref[c]
            for t, (kb, tc, b, hh) in enumerate(tiles):
                if tc == c:
                    acc = acc + pvs[t] * jnp.exp2(-r)
                    r = r + css[t][0:1, :] + ps[t][0:1, :].astype(F32)
            acc_ref[c] = acc
            r_ref[c] = r

    process([qi], True)
    n_big = qi // SB_GROUP
    left = qi - n_big * SB_GROUP

    def group(j, carry):
        first = qi - 1 - SB_GROUP * j
        process([first - u for u in range(SB_GROUP)], False)
        return carry

    lax.fori_loop(0, n_big, group, 0)
    for size in SB_TAIL_GROUPS:
        @pl.when(left // size % 2 == 1)
        def _(size=size):
            first = left % (2 * size) - 1
            process([first - u for u in range(size)], False)
    for c, (b, hh) in enumerate(chains):
        oT_ref[b, hh * HEAD_DIM:(hh + 1) * HEAD_DIM, :] = acc_ref[c].astype(BF16)


def _stick_breaking(qT, k, vT, us):
    B, nkb, width, _ = qT.shape
    S = nkb * BLK
    nb = min(ATTN_BATCH, B)
    assert B % nb == 0
    q_spec, k_spec, v_spec, o_spec = _attn_specs(nb, S, nkb)
    nc = 2 * nb
    return pl.pallas_call(
        functools.partial(_sb_kernel, nb=nb),
        grid=(B // nb, N_HEADS // 2, nkb),
        in_specs=[q_spec, k_spec, v_spec, pl.BlockSpec((BLK, BLK), lambda g, p, q: (0, 0))],
        out_specs=o_spec,
        out_shape=jax.ShapeDtypeStruct((B, width, S), BF16),
        scratch_shapes=[pltpu.VMEM((nc, PAIR, BLK), BF16),
                        pltpu.VMEM((nc, HEAD_DIM, BLK), F32),
                        pltpu.VMEM((nc, 1, BLK), F32)],
        compiler_params=pltpu.CompilerParams(
            dimension_semantics=("arbitrary", "arbitrary", "arbitrary"),
            vmem_limit_bytes=VMEM_LIMIT_BYTES),
        name="stickbreak",
    )(qT, k, vT, us)


def _bias_kernel(table_ref, bucket_ref, bias_ref):
    h = pl.program_id(0)
    bk = bucket_ref[...]
    by_dist = jnp.full(bk.shape, NEG_INF, F32)
    for i in range(REL_BUCKETS):
        by_dist = jnp.where(bk == i, table_ref[i * N_HEADS + h] * LOG2E, by_dist)
    rows = jnp.broadcast_to(by_dist, (BLK, 4 * BLK))
    skew = pltpu.roll(rows, 0, 1, stride=1, stride_axis=0)
    bias_ref[0, 0] = skew[:, BLK:2 * BLK]
    bias_ref[0, 1] = skew[:, 2 * BLK:3 * BLK]


def _t5_bucket_np(dist):
    n = np.maximum(dist, 0)
    max_exact = REL_BUCKETS // 2
    nf = np.maximum(n, 1).astype(np.float64)
    large = max_exact + (np.log(nf / max_exact) / math.log(REL_MAX_DIST / max_exact)
                         * (REL_BUCKETS - max_exact)).astype(np.int32)
    large = np.minimum(large, REL_BUCKETS - 1)
    return np.where(n < max_exact, n, large).astype(np.int32)


def _bias_tiles(rel_table):
    dist = np.arange(4 * BLK) - BLK
    buckets = np.where(dist >= 0, _t5_bucket_np(dist), -1).astype(np.int32)[None, :]
    return pl.pallas_call(
        _bias_kernel,
        grid_spec=pltpu.PrefetchScalarGridSpec(
            num_scalar_prefetch=1,
            grid=(N_HEADS,),
            in_specs=[pl.BlockSpec((1, 4 * BLK), lambda h, tab: (0, 0))],
            out_specs=pl.BlockSpec((1, 2, BLK, BLK), lambda h, tab: (h, 0, 0, 0)),
        ),
        out_shape=jax.ShapeDtypeStruct((N_HEADS, 2, BLK, BLK), F32),
        name="moba_bias",
    )(rel_table.reshape(-1), jnp.asarray(buckets))


def _moba_kernel(tab_ref, qT_ref, k_ref, vT_ref, bias_ref, oT_ref,
                 q2_ref, kmean_ref, knorm_ref, sel_ref, acc_ref, m_ref, lown_ref, *, nb, nkb, gate_scale):
    pair = pl.program_id(1)
    own = pl.program_id(2)
    chains = _chains(nb)
    _stage_pair_queries(qT_ref, q2_ref, nb)

    @pl.when(own == 0)
    def _():
        for b in range(nb):
            for n in range(nkb):
                kb = k_ref[b, n * BLK:(n + 1) * BLK, :].astype(F32)
                km = jnp.mean(kb, axis=0, keepdims=True)
                for part in range(3):
                    piece = km.astype(BF16)
                    kmean_ref[b, part * nkb + n:part * nkb + n + 1, :] = piece
                    km = km - piece.astype(F32)
        lane = lax.broadcasted_iota(jnp.int32, (PAIR, PAIR), 0) // HEAD_DIM
        same_head = (lane == lax.broadcasted_iota(jnp.int32, (PAIR, PAIR), 1) // HEAD_DIM).astype(BF16)
        for b in range(nb):
            kf = k_ref[b].astype(F32)
            norm2 = jnp.max(jnp.dot((kf * kf).astype(BF16), same_head, preferred_element_type=F32),
                            axis=0, keepdims=True)
            for hh in range(2):
                knorm_ref[2 * b + hh] = jnp.broadcast_to(
                    jnp.sqrt(norm2[:, hh * HEAD_DIM:hh * HEAD_DIM + 1]), (1, BLK))

    n_ok = jnp.minimum(own, MOBA_TOPK)
    for c, (b, hh) in enumerate(chains):
        parts = jnp.dot(kmean_ref[b], q2_ref[c], preferred_element_type=F32)
        gate = (parts[:nkb] + parts[nkb:2 * nkb] + parts[2 * nkb:]) * gate_scale
        blk = lax.broadcasted_iota(jnp.int32, gate.shape, 0)
        g = jnp.where(blk < own, gate, NEG_INF)
        sel = jnp.zeros(gate.shape, F32)
        for j in range(min(MOBA_TOPK, nkb)):
            top = jnp.max(g, axis=0, keepdims=True)
            idx = jnp.min(jnp.where(g == top, blk, nkb), axis=0, keepdims=True)
            pick = blk == idx
            sel = jnp.where(pick & (j < n_ok), 1.0, sel)
            g = jnp.where(pick, -jnp.inf, g)
        sel_ref[c] = sel

    acc_ref[...] = jnp.zeros_like(acc_ref)
    far = [tab_ref[(REL_BUCKETS - 1) * N_HEADS + 2 * pair + hh] * LOG2E for hh in range(2)]
    bias_top = []
    for hh in range(2):
        top = tab_ref[2 * pair + hh]
        for i in range(1, REL_BUCKETS):
            top = jnp.maximum(top, tab_ref[i * N_HEADS + 2 * pair + hh])
        bias_top.append(top * LOG2E)
    ones = jnp.ones((SUM_ROWS, BLK), BF16)

    def attend(blocks):
        zcs = [[jnp.dot(k_ref[b, pl.ds(pl.multiple_of(n * BLK, BLK), BLK), :], q2_ref[c],
                        preferred_element_type=F32) for n, kind in blocks]
               for c, (b, hh) in enumerate(chains)]
        pss, alphas = [], []
        for c, (b, hh) in enumerate(chains):
            m_old = m_ref[c]
            m_new = m_old
            terms = []
            for (n, kind), z in zip(blocks, zcs[c]):
                if kind == "own":
                    s, shift, chosen = z + bias_ref[hh, 0], 0.0, None
                elif kind == "prev":
                    s, shift, chosen = z + bias_ref[hh, 1], 0.0, sel_ref[c, pl.ds(n, 1), :] > 0.0
                else:
                    s, shift, chosen = z, far[hh], sel_ref[c, pl.ds(n, 1), :] > 0.0
                m_blk = jnp.max(s, axis=0, keepdims=True) + shift
                if chosen is not None:
                    m_blk = jnp.where(chosen, m_blk, NEG_INF)
                m_new = jnp.maximum(m_new, m_blk)
                terms.append((s, shift, chosen))
            alphas.append(jnp.exp2(m_old - m_new))
            ps = []
            for s, shift, chosen in terms:
                sub = m_new - shift
                if chosen is not None:
                    sub = jnp.where(chosen, sub, -NEG_INF)
                ps.append(jnp.exp2(s - sub).astype(BF16))
            m_ref[c] = m_new
            pss.append(ps)
        for c, (b, hh) in enumerate(chains):
            acc = alphas[c] * acc_ref[c]
            for (n, kind), p in zip(blocks, pss[c]):
                v1 = jnp.concatenate([vT_ref[b, n, hh * HEAD_DIM:(hh + 1) * HEAD_DIM, :], ones], axis=0)
                acc = acc + jnp.dot(v1, p, preferred_element_type=F32)
            acc_ref[c] = acc

    def attend_fixed(blocks):
        zcs = [[jnp.dot(k_ref[b, pl.ds(pl.multiple_of(n * BLK, BLK), BLK), :], q2_ref[c],
                        preferred_element_type=F32) for n, kind in blocks]
               for c, (b, hh) in enumerate(chains)]
        pss = []
        for c, (b, hh) in enumerate(chains):
            ref = m_ref[c]
            ps = []
            for (n, kind), z in zip(blocks, zcs[c]):
                if kind == "own":
                    s, sub = z + bias_ref[hh, 0], ref
                else:
                    chosen = sel_ref[c, pl.ds(n, 1), :] > 0.0
                    if kind == "prev":
                        s, sub = z + bias_ref[hh, 1], jnp.where(chosen, ref, -NEG_INF)
                    else:
                        s, sub = z, jnp.where(chosen, ref - far[hh], -NEG_INF)
                ps.append(jnp.exp2(s - sub).astype(BF16))
            pss.append(ps)
        for c, (b, hh) in enumerate(chains):
            acc = acc_ref[c]
            for (n, kind), p in zip(blocks, pss[c]):
                v1 = jnp.concatenate([vT_ref[b, n, hh * HEAD_DIM:(hh + 1) * HEAD_DIM, :], ones], axis=0)
                pv = jnp.dot(v1, p, preferred_element_type=F32)
                if kind == "own":
                    lown_ref[c] = pv[HEAD_DIM:HEAD_DIM + 1]
                acc = acc + pv
            acc_ref[c] = acc

    prev = jnp.maximum(own - 1, 0)
    n_far = jnp.maximum(own - 1, 0)

    def far_loops(step, group, tail_groups):
        n_big = n_far // group
        left = n_far - n_big * group

        def body(j, carry):
            step([group * j + u for u in range(group)])
            return carry

        lax.fori_loop(0, n_big, body, 0)
        for size in tail_groups:
            @pl.when(left // size % 2 == 1)
            def _(size=size):
                first = n_far - left % (2 * size)
                step([first + u for u in range(size)])

    for c, (b, hh) in enumerate(chains):
        q = q2_ref[c].astype(F32)
        qnorm = jnp.sqrt(jnp.sum(q * q, axis=0, keepdims=True))
        m_ref[c] = knorm_ref[c] * qnorm * NORM_SLACK + bias_top[hh]
    attend_fixed([(own, "own"), (prev, "prev")])
    far_loops(lambda ns: attend_fixed([(n, "far") for n in ns]), MOBA_FIXED_GROUP, MOBA_FIXED_TAILS)
    worst = jnp.zeros((1, BLK), F32)
    for c in range(len(chains)):
        worst = jnp.maximum(worst, jnp.where(lown_ref[c] >= FIXED_REF_MIN_SUM, 0.0, 1.0))

    @pl.when(jnp.max(worst) > 0.0)
    def _():
        m_ref[...] = jnp.full(m_ref.shape, NEG_INF, F32)
        acc_ref[...] = jnp.zeros_like(acc_ref)
        attend([(own, "own"), (prev, "prev")])
        far_loops(lambda ns: attend([(n, "far") for n in ns]), MOBA_GROUP, MOBA_TAILS)

    for c, (b, hh) in enumerate(chains):
        acc = acc_ref[c]
        oT_ref[b, hh * HEAD_DIM:(hh + 1) * HEAD_DIM, :] = (
            acc[:HEAD_DIM] / acc[HEAD_DIM:HEAD_DIM + 1]).astype(BF16)


def _moba(qT, k, vT, bias, table, *, gate_scale):
    B, nkb, width, _ = qT.shape
    S = nkb * BLK
    nb = min(ATTN_BATCH, B)
    assert B % nb == 0
    q_spec, k_spec, v_spec, o_spec = _attn_specs(nb, S, nkb, extra_args=1)
    nc = 2 * nb
    kern = functools.partial(_moba_kernel, nb=nb, nkb=nkb, gate_scale=gate_scale)
    return pl.pallas_call(
        kern,
        grid_spec=pltpu.PrefetchScalarGridSpec(
            num_scalar_prefetch=1,
            grid=(B // nb, N_HEADS // 2, nkb),
            in_specs=[q_spec, k_spec, v_spec,
                      pl.BlockSpec((2, 2, BLK, BLK), lambda g, p, q, far: (p, 0, 0, 0))],
            out_specs=o_spec,
            scratch_shapes=[pltpu.VMEM((nc, PAIR, BLK), BF16),
                            pltpu.VMEM((nb, 3 * nkb, PAIR), BF16),
                            pltpu.VMEM((nc, 1, BLK), F32),
                            pltpu.VMEM((nc, nkb, BLK), F32),
                            pltpu.VMEM((nc, HEAD_DIM + SUM_ROWS, BLK), F32),
                            pltpu.VMEM((nc, 1, BLK), F32),
                            pltpu.VMEM((nc, 1, BLK), F32)],
        ),
        out_shape=jax.ShapeDtypeStruct((B, width, S), BF16),
        compiler_params=pltpu.CompilerParams(
            dimension_semantics=("arbitrary", "arbitrary", "arbitrary"),
            vmem_limit_bytes=VMEM_LIMIT_BYTES),
        name="moba",
    )(table, qT, k, vT, bias)


def _out_kernel(oT_ref, gT_ref, x_ref, woT_ref, lng_ref, lnb_ref, h_ref, yT_ref):
    g = gT_ref[0].astype(F32)
    og = (oT_ref[0].astype(F32) * (g * (1.0 / (1.0 + jnp.exp(-g))))).astype(BF16)
    yT_ref[...] = jnp.dot(woT_ref[...], og, preferred_element_type=F32)
    r = DEEPNORM_ALPHA * x_ref[0] + yT_ref[...].T
    mu = jnp.mean(r, axis=-1, keepdims=True)
    d = r - mu
    var = jnp.mean(d * d, axis=-1, keepdims=True)
    h_ref[0] = d * lax.rsqrt(var + LN_EPS) * lng_ref[...] + lnb_ref[...]


def _out_proj_ln(oT, gT, x, woT, lng, lnb):
    B, S, D = x.shape
    width = oT.shape[1]
    rows = min(PROJ_ROWS, S)
    return pl.pallas_call(
        _out_kernel,
        grid=(B, S // rows),
        in_specs=[
            pl.BlockSpec((1, width, rows), lambda b, s: (b, 0, s)),
            pl.BlockSpec((1, width, rows), lambda b, s: (b, 0, s)),
            pl.BlockSpec((1, rows, D), lambda b, s: (b, s, 0)),
            pl.BlockSpec((D, width), lambda b, s: (0, 0)),
            pl.BlockSpec((1, D), lambda b, s: (0, 0)),
            pl.BlockSpec((1, D), lambda b, s: (0, 0)),
        ],
        out_specs=pl.BlockSpec((1, rows, D), lambda b, s: (b, s, 0)),
        out_shape=jax.ShapeDtypeStruct((B, S, D), F32),
        scratch_shapes=[pltpu.VMEM((D, rows), F32)],
        compiler_params=pltpu.CompilerParams(
            dimension_semantics=("arbitrary", "arbitrary"), vmem_limit_bytes=VMEM_LIMIT_BYTES),
        name="outproj_ln",
    )(oT, gT, x, woT, lng, lnb)


def _suffix_sum_matrix():
    return jnp.asarray(np.triu(np.ones((BLK, BLK), np.float32), k=1), BF16)


def kernel(x, w_in, w_out, ln_g, ln_b, rel_table):
    width = N_HEADS * HEAD_DIM
    scale = HEAD_DIM ** -0.5 * LOG2E
    us = _suffix_sum_matrix()
    bias = _bias_tiles(rel_table)
    h = x
    for layer in range(DEPTH):
        woT = w_out[layer].T.astype(BF16)
        k, qT, vT, gT = _project(h, w_in, layer, width=width, scale=scale)
        if layer % 2 == 0:
            oT = _stick_breaking(qT, k, vT, us)
        else:
            oT = _moba(qT, k, vT, bias, rel_table.reshape(-1), gate_scale=1.0 / scale)
        h = _out_proj_ln(oT, gT, h, woT, ln_g[layer][None, :], ln_b[layer][None, :])
    return h
```

```python
import functools
import math

import jax
import jax.numpy as jnp
import numpy as np
from jax import lax
from jax.experimental import pallas as pl
from jax.experimental.pallas import tpu as pltpu

N_HEADS = 16
HEAD_DIM = 64
DEPTH = 2
MOBA_BLOCK = 256
MOBA_TOPK = 3
REL_BUCKETS = 32
REL_MAX_DIST = 128
LN_EPS = 1e-5
DEEPNORM_ALPHA = (2.0 * DEPTH) ** 0.25
NEG_INF = -1e30
LOG2E = math.log2(math.e)
EXP2_CLAMP = 100.0

BLK = MOBA_BLOCK
PROJ_ROWS = 512
ATTN_BATCH = 4
SB_GROUP = 8
SB_TAIL_GROUPS = (4, 2, 1)
SB_LAG = (2, 4)
MOBA_GROUP, MOBA_TAILS = 2, (1,)
MOBA_FIXED_GROUP, MOBA_FIXED_TAILS = 4, (2, 1)
SUM_ROWS = 16
FIXED_MAX_MARGIN = 60.0
NORM_SLACK = 1.0 + 2.0 ** -6
VMEM_LIMIT_BYTES = 56 * 1024 * 1024

F32 = jnp.float32
BF16 = jnp.bfloat16


def _proj_kernel(x_ref, w_ref, k_ref, qT_ref, vT_ref, gT_ref, q_scr, v_scr, g_scr, *, width, scale):
    xb = x_ref[0].astype(BF16)
    n_blk = xb.shape[0] // BLK

    def part(i):
        return jnp.dot(xb, w_ref[0, :, i * width:(i + 1) * width].astype(BF16),
                       preferred_element_type=F32)

    k_ref[0] = part(1).astype(BF16)
    q_scr[...] = part(0) * scale
    v_scr[...] = part(2)
    g_scr[...] = part(3)
    qT = q_scr[...].T.astype(BF16)
    vT = v_scr[...].T.astype(BF16)
    for j in range(n_blk):
        qT_ref[0, j] = qT[:, j * BLK:(j + 1) * BLK]
        vT_ref[0, j] = vT[:, j * BLK:(j + 1) * BLK]
    gT_ref[0] = g_scr[...].T.astype(BF16)


def _project(x, w_in, layer, *, width, scale):
    B, S, D = x.shape
    rows = min(PROJ_ROWS, S)
    assert S % rows == 0 and rows % BLK == 0
    n_blk = rows // BLK
    nkb = S // BLK
    kern = functools.partial(_proj_kernel, width=width, scale=scale)
    return pl.pallas_call(
        kern,
        grid=(B, S // rows),
        in_specs=[
            pl.BlockSpec((1, rows, D), lambda b, s: (b, s, 0)),
            pl.BlockSpec((1, D, 4 * width), lambda b, s: (layer, 0, 0), pipeline_mode=pl.Buffered(1)),
        ],
        out_specs=[
            pl.BlockSpec((1, rows, width), lambda b, s: (b, s, 0)),
            pl.BlockSpec((1, n_blk, width, BLK), lambda b, s: (b, s, 0, 0)),
            pl.BlockSpec((1, n_blk, width, BLK), lambda b, s: (b, s, 0, 0)),
            pl.BlockSpec((1, width, rows), lambda b, s: (b, 0, s)),
        ],
        out_shape=[
            jax.ShapeDtypeStruct((B, S, width), BF16),
            jax.ShapeDtypeStruct((B, nkb, width, BLK), BF16),
            jax.ShapeDtypeStruct((B, nkb, width, BLK), BF16),
            jax.ShapeDtypeStruct((B, width, S), BF16),
        ],
        scratch_shapes=[pltpu.VMEM((rows, width), F32)] * 3,
        compiler_params=pltpu.CompilerParams(
            dimension_semantics=("arbitrary", "arbitrary"), vmem_limit_bytes=VMEM_LIMIT_BYTES),
        name="proj",
    )(x, w_in)


PAIR = 2 * HEAD_DIM


def _chains(nb):
    return [(b, hh) for b in range(nb) for hh in range(2)]


def _stage_pair_queries(qT_ref, q2_ref, nb):
    zeros = jnp.zeros((HEAD_DIM, BLK), BF16)
    for c, (b, hh) in enumerate(_chains(nb)):
        q = qT_ref[b, 0, hh * HEAD_DIM:(hh + 1) * HEAD_DIM, :]
        q2_ref[c] = jnp.concatenate([q, zeros] if hh == 0 else [zeros, q], axis=0)


def _attn_specs(nb, S, nkb, extra_args=0):
    def im(f):
        if extra_args:
            return lambda g, p, q, *_: f(g, p, q)
        return f
    q_spec = pl.BlockSpec((nb, 1, PAIR, BLK), im(lambda g, p, q: (g, q, p, 0)))
    k_spec = pl.BlockSpec((nb, S, PAIR), im(lambda g, p, q: (g, 0, p)))
    v_spec = pl.BlockSpec((nb, nkb, PAIR, BLK), im(lambda g, p, q: (g, 0, p, 0)))
    o_spec = pl.BlockSpec((nb, PAIR, BLK), im(lambda g, p, q: (g, p, q)))
    return q_spec, k_spec, v_spec, o_spec


def _sb_kernel(qT_ref, k_ref, vT_ref, us_ref, oT_ref, q2_ref, acc_ref, r_ref, *, nb):
    qi = pl.program_id(2)
    chains = _chains(nb)
    _stage_pair_queries(qT_ref, q2_ref, nb)
    acc_ref[...] = jnp.zeros_like(acc_ref)
    r_ref[...] = jnp.zeros_like(r_ref)

    def process(kbs, masked):
        if masked:
            row = lax.broadcasted_iota(jnp.int32, (BLK, BLK), 0)
            col = lax.broadcasted_iota(jnp.int32, (BLK, BLK), 1)
            past = row < col
        tiles = [(kb, c, b, hh) for kb in kbs for c, (b, hh) in enumerate(chains)]
        n_tiles = len(tiles)
        zs, ys, ps, css, pvs = {}, {}, {}, {}, {}
        lag_p, lag_w = SB_LAG
        for step in range(n_tiles + lag_w):
            if step < n_tiles:
                kb, c, b, hh = tiles[step]
                zs[step] = jnp.dot(k_ref[b, pl.ds(pl.multiple_of(kb * BLK, BLK), BLK), :], q2_ref[c],
                                   preferred_element_type=F32)
            i = step - lag_p
            if 0 <= i < n_tiles:
                z = zs.pop(i)
                p = jnp.maximum(jnp.log(1.0 + jnp.exp2(jnp.minimum(z, EXP2_CLAMP))) * LOG2E, z)
                if masked:
                    p = jnp.where(past, p, 0.0)
                ys[i] = z - p
                ps[i] = p.astype(BF16)
                css[i] = jnp.dot(us_ref[...], ps[i], preferred_element_type=F32)
            j = step - lag_w
            if 0 <= j < n_tiles:
                kb, c, b, hh = tiles[j]
                w = jnp.exp2(ys.pop(j) - css[j])
                if masked:
                    w = jnp.where(past, w, 0.0)
                pvs[j] = jnp.dot(vT_ref[b, kb, hh * HEAD_DIM:(hh + 1) * HEAD_DIM, :], w.astype(BF16),
                                 preferred_element_type=F32)
        for c in range(len(chains)):
            r = r_ref[c]
            acc = acc_ref[c]
            for t, (kb, tc, b, hh) in enumerate(tiles):
                if tc == c:
                    acc = acc + pvs[t] * jnp.exp2(-r)
                    r = r + css[t][0:1, :] + ps[t][0:1, :].astype(F32)
            acc_ref[c] = acc
            r_ref[c] = r

    process([qi], True)
    n_big = qi // SB_GROUP
    left = qi - n_big * SB_GROUP

    def group(j, carry):
        first = qi - 1 - SB_GROUP * j
        process([first - u for u in range(SB_GROUP)], False)
        return carry

    lax.fori_loop(0, n_big, group, 0)
    for size in SB_TAIL_GROUPS:
        @pl.when(left // size % 2 == 1)
        def _(size=size):
            first = left % (2 * size) - 1
            process([first - u for u in range(size)], False)
    for c, (b, hh) in enumerate(chains):
        oT_ref[b, hh * HEAD_DIM:(hh + 1) * HEAD_DIM, :] = acc_ref[c].astype(BF16)


def _stick_breaking(qT, k, vT, us):
    B, nkb, width, _ = qT.shape
    S = nkb * BLK
    nb = min(ATTN_BATCH, B)
    assert B % nb == 0
    q_spec, k_spec, v_spec, o_spec = _attn_specs(nb, S, nkb)
    nc = 2 * nb
    return pl.pallas_call(
        functools.partial(_sb_kernel, nb=nb),
        grid=(B // nb, N_HEADS // 2, nkb),
        in_specs=[q_spec, k_spec, v_spec, pl.BlockSpec((BLK, BLK), lambda g, p, q: (0, 0))],
        out_specs=o_spec,
        out_shape=jax.ShapeDtypeStruct((B, width, S), BF16),
        scratch_shapes=[pltpu.VMEM((nc, PAIR, BLK), BF16),
                        pltpu.VMEM((nc, HEAD_DIM, BLK), F32),
                        pltpu.VMEM((nc, 1, BLK), F32)],
        compiler_params=pltpu.CompilerParams(
            dimension_semantics=("arbitrary", "arbitrary", "arbitrary"),
            vmem_limit_bytes=VMEM_LIMIT_BYTES),
        name="stickbreak",
    )(qT, k, vT, us)


def _bias_kernel(table_ref, bucket_ref, bias_ref):
    h = pl.program_id(0)
    bk = bucket_ref[...]
    by_dist = jnp.full(bk.shape, NEG_INF, F32)
    for i in range(REL_BUCKETS):
        by_dist = jnp.where(bk == i, table_ref[i * N_HEADS + h] * LOG2E, by_dist)
    rows = jnp.broadcast_to(by_dist, (BLK, 4 * BLK))
    skew = pltpu.roll(rows, 0, 1, stride=1, stride_axis=0)
    bias_ref[0, 0] = skew[:, BLK:2 * BLK]
    bias_ref[0, 1] = skew[:, 2 * BLK:3 * BLK]


def _t5_bucket_np(dist):
    n = np.maximum(dist, 0)
    max_exact = REL_BUCKETS // 2
    nf = np.maximum(n, 1).astype(np.float64)
    large = max_exact + (np.log(nf / max_exact) / math.log(REL_MAX_DIST / max_exact)
                         * (REL_BUCKETS - max_exact)).astype(np.int32)
    large = np.minimum(large, REL_BUCKETS - 1)
    return np.where(n < max_exact, n, large).astype(np.int32)


def _bias_tiles(rel_table):
    dist = np.arange(4 * BLK) - BLK
    buckets = np.where(dist >= 0, _t5_bucket_np(dist), -1).astype(np.int32)[None, :]
    return pl.pallas_call(
        _bias_kernel,
        grid_spec=pltpu.PrefetchScalarGridSpec(
            num_scalar_prefetch=1,
            grid=(N_HEADS,),
            in_specs=[pl.BlockSpec((1, 4 * BLK), lambda h, tab: (0, 0))],
            out_specs=pl.BlockSpec((1, 2, BLK, BLK), lambda h, tab: (h, 0, 0, 0)),
        ),
        out_shape=jax.ShapeDtypeStruct((N_HEADS, 2, BLK, BLK), F32),
        name="moba_bias",
    )(rel_table.reshape(-1), jnp.asarray(buckets))


def _moba_kernel(far_ref, qT_ref, k_ref, vT_ref, bias_ref, oT_ref,
                 q2_ref, kmean_ref, knorm_ref, sel_ref, acc_ref, m_ref, *, nb, nkb, gate_scale):
    pair = pl.program_id(1)
    own = pl.program_id(2)
    chains = _chains(nb)
    _stage_pair_queries(qT_ref, q2_ref, nb)

    @pl.when(own == 0)
    def _():
        for b in range(nb):
            for n in range(nkb):
                kb = k_ref[b, n * BLK:(n + 1) * BLK, :].astype(F32)
                km = jnp.mean(kb, axis=0, keepdims=True)
                for part in range(3):
                    piece = km.astype(BF16)
                    kmean_ref[b, part * nkb + n:part * nkb + n + 1, :] = piece
                    km = km - piece.astype(F32)
        lane = lax.broadcasted_iota(jnp.int32, (PAIR, PAIR), 0) // HEAD_DIM
        same_head = (lane == lax.broadcasted_iota(jnp.int32, (PAIR, PAIR), 1) // HEAD_DIM).astype(BF16)
        for b in range(nb):
            kf = k_ref[b].astype(F32)
            norm2 = jnp.max(jnp.dot((kf * kf).astype(BF16), same_head, preferred_element_type=F32),
                            axis=0, keepdims=True)
            for hh in range(2):
                knorm_ref[2 * b + hh] = jnp.broadcast_to(
                    jnp.sqrt(norm2[:, hh * HEAD_DIM:hh * HEAD_DIM + 1]), (1, BLK))

    n_ok = jnp.minimum(own, MOBA_TOPK)
    for c, (b, hh) in enumerate(chains):
        parts = jnp.dot(kmean_ref[b], q2_ref[c], preferred_element_type=F32)
        gate = (parts[:nkb] + parts[nkb:2 * nkb] + parts[2 * nkb:]) * gate_scale
        blk = lax.broadcasted_iota(jnp.int32, gate.shape, 0)
        g = jnp.where(blk < own, gate, NEG_INF)
        sel = jnp.zeros(gate.shape, F32)
        for j in range(min(MOBA_TOPK, nkb)):
            top = jnp.max(g, axis=0, keepdims=True)
            idx = jnp.min(jnp.where(g == top, blk, nkb), axis=0, keepdims=True)
            pick = blk == idx
            sel = jnp.where(pick & (j < n_ok), 1.0, sel)
            g = jnp.where(pick, -jnp.inf, g)
        sel_ref[c] = sel

    m_ref[...] = jnp.full(m_ref.shape, NEG_INF, F32)
    acc_ref[...] = jnp.zeros_like(acc_ref)
    far = [far_ref[2 * pair + hh] * LOG2E for hh in range(2)]
    ones = jnp.ones((SUM_ROWS, BLK), BF16)

    def attend(blocks):
        zcs = [[jnp.dot(k_ref[b, pl.ds(pl.multiple_of(n * BLK, BLK), BLK), :], q2_ref[c],
                        preferred_element_type=F32) for n, kind in blocks]
               for c, (b, hh) in enumerate(chains)]
        pss, alphas = [], []
        for c, (b, hh) in enumerate(chains):
            m_old = m_ref[c]
            m_new = m_old
            terms = []
            for (n, kind), z in zip(blocks, zcs[c]):
                if kind == "own":
                    s, shift, chosen = z + bias_ref[hh, 0], 0.0, None
                elif kind == "prev":
                    s, shift, chosen = z + bias_ref[hh, 1], 0.0, sel_ref[c, pl.ds(n, 1), :] > 0.0
                else:
                    s, shift, chosen = z, far[hh], sel_ref[c, pl.ds(n, 1), :] > 0.0
                m_blk = jnp.max(s, axis=0, keepdims=True) + shift
                if chosen is not None:
                    m_blk = jnp.where(chosen, m_blk, NEG_INF)
                m_new = jnp.maximum(m_new, m_blk)
                terms.append((s, shift, chosen))
            alphas.append(jnp.exp2(m_old - m_new))
            ps = []
            for s, shift, chosen in terms:
                sub = m_new - shift
                if chosen is not None:
                    sub = jnp.where(chosen, sub, -NEG_INF)
                ps.append(jnp.exp2(s - sub).astype(BF16))
            m_ref[c] = m_new
            pss.append(ps)
        for c, (b, hh) in enumerate(chains):
            acc = alphas[c] * acc_ref[c]
            for (n, kind), p in zip(blocks, pss[c]):
                v1 = jnp.concatenate([vT_ref[b, n, hh * HEAD_DIM:(hh + 1) * HEAD_DIM, :], ones], axis=0)
                acc = acc + jnp.dot(v1, p, preferred_element_type=F32)
            acc_ref[c] = acc

    def attend_fixed(ns):
        zcs = [[jnp.dot(k_ref[b, pl.ds(pl.multiple_of(n * BLK, BLK), BLK), :], q2_ref[c],
                        preferred_element_type=F32) for n in ns]
               for c, (b, hh) in enumerate(chains)]
        pss = []
        for c, (b, hh) in enumerate(chains):
            ref = m_ref[c] - far[hh]
            ps = []
            for n, z in zip(ns, zcs[c]):
                chosen = sel_ref[c, pl.ds(n, 1), :] > 0.0
                ps.append(jnp.exp2(z - jnp.where(chosen, ref, -NEG_INF)).astype(BF16))
            pss.append(ps)
        for c, (b, hh) in enumerate(chains):
            acc = acc_ref[c]
            for n, p in zip(ns, pss[c]):
                v1 = jnp.concatenate([vT_ref[b, n, hh * HEAD_DIM:(hh + 1) * HEAD_DIM, :], ones], axis=0)
                acc = acc + jnp.dot(v1, p, preferred_element_type=F32)
            acc_ref[c] = acc

    attend([(own, "own"), (jnp.maximum(own - 1, 0), "prev")])
    n_far = jnp.maximum(own - 1, 0)

    worst = jnp.zeros((1, BLK), F32)
    for c, (b, hh) in enumerate(chains):
        q = q2_ref[c].astype(F32)
        qnorm = jnp.sqrt(jnp.sum(q * q, axis=0, keepdims=True))
        bound = knorm_ref[c] * qnorm * NORM_SLACK + far[hh]
        worst = jnp.maximum(worst, jnp.where(bound <= m_ref[c] + FIXED_MAX_MARGIN, 0.0, 1.0))
    fixed_ok = jnp.max(worst) == 0.0

    def far_loops(step, group, tail_groups):
        n_big = n_far // group
        left = n_far - n_big * group

        def body(j, carry):
            step([group * j + u for u in range(group)])
            return carry

        lax.fori_loop(0, n_big, body, 0)
        for size in tail_groups:
            @pl.when(left // size % 2 == 1)
            def _(size=size):
                first = n_far - left % (2 * size)
                step([first + u for u in range(size)])

    @pl.when(fixed_ok)
    def _():
        far_loops(attend_fixed, MOBA_FIXED_GROUP, MOBA_FIXED_TAILS)

    @pl.when(jnp.logical_not(fixed_ok))
    def _():
        far_loops(lambda ns: attend([(n, "far") for n in ns]), MOBA_GROUP, MOBA_TAILS)

    for c, (b, hh) in enumerate(chains):
        acc = acc_ref[c]
        oT_ref[b, hh * HEAD_DIM:(hh + 1) * HEAD_DIM, :] = (
            acc[:HEAD_DIM] / acc[HEAD_DIM:HEAD_DIM + 1]).astype(BF16)


def _moba(qT, k, vT, bias, far, *, gate_scale):
    B, nkb, width, _ = qT.shape
    S = nkb * BLK
    nb = min(ATTN_BATCH, B)
    assert B % nb == 0
    q_spec, k_spec, v_spec, o_spec = _attn_specs(nb, S, nkb, extra_args=1)
    nc = 2 * nb
    kern = functools.partial(_moba_kernel, nb=nb, nkb=nkb, gate_scale=gate_scale)
    return pl.pallas_call(
        kern,
        grid_spec=pltpu.PrefetchScalarGridSpec(
            num_scalar_prefetch=1,
            grid=(B // nb, N_HEADS // 2, nkb),
            in_specs=[q_spec, k_spec, v_spec,
                      pl.BlockSpec((2, 2, BLK, BLK), lambda g, p, q, far: (p, 0, 0, 0))],
            out_specs=o_spec,
            scratch_shapes=[pltpu.VMEM((nc, PAIR, BLK), BF16),
                            pltpu.VMEM((nb, 3 * nkb, PAIR), BF16),
                            pltpu.VMEM((nc, 1, BLK), F32),
                            pltpu.VMEM((nc, nkb, BLK), F32),
                            pltpu.VMEM((nc, HEAD_DIM + SUM_ROWS, BLK), F32),
                            pltpu.VMEM((nc, 1, BLK), F32)],
        ),
        out_shape=jax.ShapeDtypeStruct((B, width, S), BF16),
        compiler_params=pltpu.CompilerParams(
            dimension_semantics=("arbitrary", "arbitrary", "arbitrary"),
            vmem_limit_bytes=VMEM_LIMIT_BYTES),
        name="moba",
    )(far, qT, k, vT, bias)


def _out_kernel(oT_ref, gT_ref, x_ref, woT_ref, lng_ref, lnb_ref, h_ref, yT_ref):
    g = gT_ref[0].astype(F32)
    og = (oT_ref[0].astype(F32) * (g * (1.0 / (1.0 + jnp.exp(-g))))).astype(BF16)
    yT_ref[...] = jnp.dot(woT_ref[...], og, preferred_element_type=F32)
    r = DEEPNORM_ALPHA * x_ref[0] + yT_ref[...].T
    mu = jnp.mean(r, axis=-1, keepdims=True)
    d = r - mu
    var = jnp.mean(d * d, axis=-1, keepdims=True)
    h_ref[0] = d * lax.rsqrt(var + LN_EPS) * lng_ref[...] + lnb_ref[...]


def _out_proj_ln(oT, gT, x, woT, lng, lnb):
    B, S, D = x.shape
    width = oT.shape[1]
    rows = min(PROJ_ROWS, S)
    return pl.pallas_call(
        _out_kernel,
        grid=(B, S // rows),
        in_specs=[
            pl.BlockSpec((1, width, rows), lambda b, s: (b, 0, s)),
            pl.BlockSpec((1, width, rows), lambda b, s: (b, 0, s)),
            pl.BlockSpec((1, rows, D), lambda b, s: (b, s, 0)),
            pl.BlockSpec((D, width), lambda b, s: (0, 0)),
            pl.BlockSpec((1, D), lambda b, s: (0, 0)),
            pl.BlockSpec((1, D), lambda b, s: (0, 0)),
        ],
        out_specs=pl.BlockSpec((1, rows, D), lambda b, s: (b, s, 0)),
        out_shape=jax.ShapeDtypeStruct((B, S, D), F32),
        scratch_shapes=[pltpu.VMEM((D, rows), F32)],
        compiler_params=pltpu.CompilerParams(
            dimension_semantics=("arbitrary", "arbitrary"), vmem_limit_bytes=VMEM_LIMIT_BYTES),
        name="outproj_ln",
    )(oT, gT, x, woT, lng, lnb)


def _suffix_sum_matrix():
    return jnp.asarray(np.triu(np.ones((BLK, BLK), np.float32), k=1), BF16)


def kernel(x, w_in, w_out, ln_g, ln_b, rel_table):
    width = N_HEADS * HEAD_DIM
    scale = HEAD_DIM ** -0.5 * LOG2E
    us = _suffix_sum_matrix()
    bias = _bias_tiles(rel_table)
    far = rel_table[REL_BUCKETS - 1]
    h = x
    for layer in range(DEPTH):
        woT = w_out[layer].T.astype(BF16)
        k, qT, vT, gT = _project(h, w_in, layer, width=width, scale=scale)
        if layer % 2 == 0:
            oT = _stick_breaking(qT, k, vT, us)
        else:
            oT = _moba(qT, k, vT, bias, far, gate_scale=1.0 / scale)
        h = _out_proj_ln(oT, gT, h, woT, ln_g[layer][None, :], ln_b[layer][None, :])
    return h
```

```python
import functools
import math

import jax
import jax.numpy as jnp
import numpy as np
from jax import lax
from jax.experimental import pallas as pl
from jax.experimental.pallas import tpu as pltpu

N_HEADS = 16
HEAD_DIM = 64
DEPTH = 2
MOBA_BLOCK = 256
MOBA_TOPK = 3
REL_BUCKETS = 32
REL_MAX_DIST = 128
LN_EPS = 1e-5
DEEPNORM_ALPHA = (2.0 * DEPTH) ** 0.25
NEG_INF = -1e30
LOG2E = math.log2(math.e)
EXP2_CLAMP = 100.0

BLK = MOBA_BLOCK
PROJ_ROWS = 512
ATTN_BATCH = 4
SB_GROUP = 4
SB_TAIL_GROUPS = (2, 1)
SB_LAG = (2, 4)
MOBA_GROUP, MOBA_TAILS = 2, (1,)
MOBA_FIXED_GROUP, MOBA_FIXED_TAILS = 4, (2, 1)
SUM_ROWS = 16
FIXED_MAX_MARGIN = 60.0
NORM_SLACK = 1.0 + 2.0 ** -6
VMEM_LIMIT_BYTES = 56 * 1024 * 1024

F32 = jnp.float32
BF16 = jnp.bfloat16


def _proj_kernel(x_ref, w_ref, k_ref, qT_ref, vT_ref, gT_ref, q_scr, v_scr, g_scr, *, width, scale):
    xb = x_ref[0].astype(BF16)
    n_blk = xb.shape[0] // BLK

    def part(i):
        return jnp.dot(xb, w_ref[0, :, i * width:(i + 1) * width].astype(BF16),
                       preferred_element_type=F32)

    k_ref[0] = part(1).astype(BF16)
    q_scr[...] = part(0) * scale
    v_scr[...] = part(2)
    g_scr[...] = part(3)
    qT = q_scr[...].T.astype(BF16)
    vT = v_scr[...].T.astype(BF16)
    for j in range(n_blk):
        qT_ref[0, j] = qT[:, j * BLK:(j + 1) * BLK]
        vT_ref[0, j] = vT[:, j * BLK:(j + 1) * BLK]
    gT_ref[0] = g_scr[...].T.astype(BF16)


def _project(x, w_in, layer, *, width, scale):
    B, S, D = x.shape
    rows = min(PROJ_ROWS, S)
    assert S % rows == 0 and rows % BLK == 0
    n_blk = rows // BLK
    nkb = S // BLK
    kern = functools.partial(_proj_kernel, width=width, scale=scale)
    return pl.pallas_call(
        kern,
        grid=(B, S // rows),
        in_specs=[
            pl.BlockSpec((1, rows, D), lambda b, s: (b, s, 0)),
            pl.BlockSpec((1, D, 4 * width), lambda b, s: (layer, 0, 0), pipeline_mode=pl.Buffered(1)),
        ],
        out_specs=[
            pl.BlockSpec((1, rows, width), lambda b, s: (b, s, 0)),
            pl.BlockSpec((1, n_blk, width, BLK), lambda b, s: (b, s, 0, 0)),
            pl.BlockSpec((1, n_blk, width, BLK), lambda b, s: (b, s, 0, 0)),
            pl.BlockSpec((1, width, rows), lambda b, s: (b, 0, s)),
        ],
        out_shape=[
            jax.ShapeDtypeStruct((B, S, width), BF16),
            jax.ShapeDtypeStruct((B, nkb, width, BLK), BF16),
            jax.ShapeDtypeStruct((B, nkb, width, BLK), BF16),
            jax.ShapeDtypeStruct((B, width, S), BF16),
        ],
        scratch_shapes=[pltpu.VMEM((rows, width), F32)] * 3,
        compiler_params=pltpu.CompilerParams(
            dimension_semantics=("arbitrary", "arbitrary"), vmem_limit_bytes=VMEM_LIMIT_BYTES),
        name="proj",
    )(x, w_in)


SB_HEADS, MOBA_HEADS = 4, 2


def _chains(nb, hps):
    return [(b, hh) for b in range(nb) for hh in range(hps)]


def _stage_pair_queries(qT_ref, q2_ref, nb, hps):
    zeros = jnp.zeros((HEAD_DIM, BLK), BF16)
    for c, (b, hh) in enumerate(_chains(nb, hps)):
        q = qT_ref[b, 0, hh * HEAD_DIM:(hh + 1) * HEAD_DIM, :]
        q2_ref[c] = jnp.concatenate([zeros] * hh + [q] + [zeros] * (hps - 1 - hh), axis=0)


def _attn_specs(nb, S, nkb, hps, extra_args=0):
    PAIR = hps * HEAD_DIM
    def im(f):
        if extra_args:
            return lambda g, p, q, *_: f(g, p, q)
        return f
    q_spec = pl.BlockSpec((nb, 1, PAIR, BLK), im(lambda g, p, q: (g, q, p, 0)))
    k_spec = pl.BlockSpec((nb, S, PAIR), im(lambda g, p, q: (g, 0, p)))
    v_spec = pl.BlockSpec((nb, nkb, PAIR, BLK), im(lambda g, p, q: (g, 0, p, 0)))
    o_spec = pl.BlockSpec((nb, PAIR, BLK), im(lambda g, p, q: (g, p, q)))
    return q_spec, k_spec, v_spec, o_spec


def _sb_kernel(qT_ref, k_ref, vT_ref, us_ref, oT_ref, q2_ref, acc_ref, r_ref, *, nb):
    qi = pl.program_id(2)
    chains = _chains(nb, SB_HEADS)
    _stage_pair_queries(qT_ref, q2_ref, nb, SB_HEADS)
    acc_ref[...] = jnp.zeros_like(acc_ref)
    r_ref[...] = jnp.zeros_like(r_ref)

    def process(kbs, masked):
        if masked:
            row = lax.broadcasted_iota(jnp.int32, (BLK, BLK), 0)
            col = lax.broadcasted_iota(jnp.int32, (BLK, BLK), 1)
            past = row < col
        tiles = [(kb, c, b, hh) for kb in kbs for c, (b, hh) in enumerate(chains)]
        n_tiles = len(tiles)
        zs, ys, ps, css, pvs = {}, {}, {}, {}, {}
        lag_p, lag_w = SB_LAG
        for step in range(n_tiles + lag_w):
            if step < n_tiles:
                kb, c, b, hh = tiles[step]
                zs[step] = jnp.dot(k_ref[b, pl.ds(pl.multiple_of(kb * BLK, BLK), BLK), :], q2_ref[c],
                                   preferred_element_type=F32)
            i = step - lag_p
            if 0 <= i < n_tiles:
                z = zs.pop(i)
                p = jnp.maximum(jnp.log(1.0 + jnp.exp2(jnp.minimum(z, EXP2_CLAMP))) * LOG2E, z)
                if masked:
                    p = jnp.where(past, p, 0.0)
                ys[i] = z - p
                ps[i] = p.astype(BF16)
                css[i] = jnp.dot(us_ref[...], ps[i], preferred_element_type=F32)
            j = step - lag_w
            if 0 <= j < n_tiles:
                kb, c, b, hh = tiles[j]
                w = jnp.exp2(ys.pop(j) - css[j])
                if masked:
                    w = jnp.where(past, w, 0.0)
                pvs[j] = jnp.dot(vT_ref[b, kb, hh * HEAD_DIM:(hh + 1) * HEAD_DIM, :], w.astype(BF16),
                                 preferred_element_type=F32)
        for c in range(len(chains)):
            r = r_ref[c]
            acc = acc_ref[c]
            for t, (kb, tc, b, hh) in enumerate(tiles):
                if tc == c:
                    acc = acc + pvs[t] * jnp.exp2(-r)
                    r = r + css[t][0:1, :] + ps[t][0:1, :].astype(F32)
            acc_ref[c] = acc
            r_ref[c] = r

    process([qi], True)
    n_big = qi // SB_GROUP
    left = qi - n_big * SB_GROUP

    def group(j, carry):
        first = qi - 1 - SB_GROUP * j
        process([first - u for u in range(SB_GROUP)], False)
        return carry

    lax.fori_loop(0, n_big, group, 0)
    for size in SB_TAIL_GROUPS:
        @pl.when(left // size % 2 == 1)
        def _(size=size):
            first = left % (2 * size) - 1
            process([first - u for u in range(size)], False)
    for c, (b, hh) in enumerate(chains):
        oT_ref[b, hh * HEAD_DIM:(hh + 1) * HEAD_DIM, :] = acc_ref[c].astype(BF16)


def _stick_breaking(qT, k, vT, us):
    B, nkb, width, _ = qT.shape
    S = nkb * BLK
    nb = min(ATTN_BATCH, B)
    assert B % nb == 0
    q_spec, k_spec, v_spec, o_spec = _attn_specs(nb, S, nkb, SB_HEADS)
    nc = SB_HEADS * nb
    PAIR = SB_HEADS * HEAD_DIM
    return pl.pallas_call(
        functools.partial(_sb_kernel, nb=nb),
        grid=(B // nb, N_HEADS // SB_HEADS, nkb),
        in_specs=[q_spec, k_spec, v_spec, pl.BlockSpec((BLK, BLK), lambda g, p, q: (0, 0))],
        out_specs=o_spec,
        out_shape=jax.ShapeDtypeStruct((B, width, S), BF16),
        scratch_shapes=[pltpu.VMEM((nc, PAIR, BLK), BF16),
                        pltpu.VMEM((nc, HEAD_DIM, BLK), F32),
                        pltpu.VMEM((nc, 1, BLK), F32)],
        compiler_params=pltpu.CompilerParams(
            dimension_semantics=("arbitrary", "arbitrary", "arbitrary"),
            vmem_limit_bytes=VMEM_LIMIT_BYTES),
        name="stickbreak",
    )(qT, k, vT, us)


def _bias_kernel(table_ref, bucket_ref, bias_ref):
    h = pl.program_id(0)
    bk = bucket_ref[...]
    by_dist = jnp.full(bk.shape, NEG_INF, F32)
    for i in range(REL_BUCKETS):
        by_dist = jnp.where(bk == i, table_ref[i * N_HEADS + h] * LOG2E, by_dist)
    rows = jnp.broadcast_to(by_dist, (BLK, 4 * BLK))
    skew = pltpu.roll(rows, 0, 1, stride=1, stride_axis=0)
    bias_ref[0, 0] = skew[:, BLK:2 * BLK]
    bias_ref[0, 1] = skew[:, 2 * BLK:3 * BLK]


def _t5_bucket_np(dist):
    n = np.maximum(dist, 0)
    max_exact = REL_BUCKETS // 2
    nf = np.maximum(n, 1).astype(np.float64)
    large = max_exact + (np.log(nf / max_exact) / math.log(REL_MAX_DIST / max_exact)
                         * (REL_BUCKETS - max_exact)).astype(np.int32)
    large = np.minimum(large, REL_BUCKETS - 1)
    return np.where(n < max_exact, n, large).astype(np.int32)


def _bias_tiles(rel_table):
    dist = np.arange(4 * BLK) - BLK
    buckets = np.where(dist >= 0, _t5_bucket_np(dist), -1).astype(np.int32)[None, :]
    return pl.pallas_call(
        _bias_kernel,
        grid_spec=pltpu.PrefetchScalarGridSpec(
            num_scalar_prefetch=1,
            grid=(N_HEADS,),
            in_specs=[pl.BlockSpec((1, 4 * BLK), lambda h, tab: (0, 0))],
            out_specs=pl.BlockSpec((1, 2, BLK, BLK), lambda h, tab: (h, 0, 0, 0)),
        ),
        out_shape=jax.ShapeDtypeStruct((N_HEADS, 2, BLK, BLK), F32),
        name="moba_bias",
    )(rel_table.reshape(-1), jnp.asarray(buckets))


def _moba_kernel(far_ref, qT_ref, k_ref, vT_ref, bias_ref, oT_ref,
                 q2_ref, kmean_ref, knorm_ref, sel_ref, acc_ref, m_ref, *, nb, nkb, gate_scale):
    pair = pl.program_id(1)
    own = pl.program_id(2)
    chains = _chains(nb, MOBA_HEADS)
    PAIR = MOBA_HEADS * HEAD_DIM
    _stage_pair_queries(qT_ref, q2_ref, nb, MOBA_HEADS)

    @pl.when(own == 0)
    def _():
        for b in range(nb):
            for n in range(nkb):
                kb = k_ref[b, n * BLK:(n + 1) * BLK, :].astype(F32)
                km = jnp.mean(kb, axis=0, keepdims=True)
                for part in range(3):
                    piece = km.astype(BF16)
                    kmean_ref[b, part * nkb + n:part * nkb + n + 1, :] = piece
                    km = km - piece.astype(F32)
        lane = lax.broadcasted_iota(jnp.int32, (PAIR, PAIR), 0) // HEAD_DIM
        same_head = (lane == lax.broadcasted_iota(jnp.int32, (PAIR, PAIR), 1) // HEAD_DIM).astype(BF16)
        for b in range(nb):
            kf = k_ref[b].astype(F32)
            norm2 = jnp.max(jnp.dot((kf * kf).astype(BF16), same_head, preferred_element_type=F32),
                            axis=0, keepdims=True)
            for hh in range(MOBA_HEADS):
                knorm_ref[MOBA_HEADS * b + hh] = jnp.broadcast_to(
                    jnp.sqrt(norm2[:, hh * HEAD_DIM:hh * HEAD_DIM + 1]), (1, BLK))

    n_ok = jnp.minimum(own, MOBA_TOPK)
    for c, (b, hh) in enumerate(chains):
        parts = jnp.dot(kmean_ref[b], q2_ref[c], preferred_element_type=F32)
        gate = (parts[:nkb] + parts[nkb:2 * nkb] + parts[2 * nkb:]) * gate_scale
        blk = lax.broadcasted_iota(jnp.int32, gate.shape, 0)
        g = jnp.where(blk < own, gate, NEG_INF)
        sel = jnp.zeros(gate.shape, F32)
        for j in range(min(MOBA_TOPK, nkb)):
            top = jnp.max(g, axis=0, keepdims=True)
            idx = jnp.min(jnp.where(g == top, blk, nkb), axis=0, keepdims=True)
            pick = blk == idx
            sel = jnp.where(pick & (j < n_ok), 1.0, sel)
            g = jnp.where(pick, -jnp.inf, g)
        sel_ref[c] = sel

    m_ref[...] = jnp.full(m_ref.shape, NEG_INF, F32)
    acc_ref[...] = jnp.zeros_like(acc_ref)
    far = [far_ref[MOBA_HEADS * pair + hh] * LOG2E for hh in range(MOBA_HEADS)]
    ones = jnp.ones((SUM_ROWS, BLK), BF16)

    def attend(blocks):
        zcs = [[jnp.dot(k_ref[b, pl.ds(pl.multiple_of(n * BLK, BLK), BLK), :], q2_ref[c],
                        preferred_element_type=F32) for n, kind in blocks]
               for c, (b, hh) in enumerate(chains)]
        pss, alphas = [], []
        for c, (b, hh) in enumerate(chains):
            m_old = m_ref[c]
            m_new = m_old
            terms = []
            for (n, kind), z in zip(blocks, zcs[c]):
                if kind == "own":
                    s, shift, chosen = z + bias_ref[hh, 0], 0.0, None
                elif kind == "prev":
                    s, shift, chosen = z + bias_ref[hh, 1], 0.0, sel_ref[c, pl.ds(n, 1), :] > 0.0
                else:
                    s, shift, chosen = z, far[hh], sel_ref[c, pl.ds(n, 1), :] > 0.0
                m_blk = jnp.max(s, axis=0, keepdims=True) + shift
                if chosen is not None:
                    m_blk = jnp.where(chosen, m_blk, NEG_INF)
                m_new = jnp.maximum(m_new, m_blk)
                terms.append((s, shift, chosen))
            alphas.append(jnp.exp2(m_old - m_new))
            ps = []
            for s, shift, chosen in terms:
                sub = m_new - shift
                if chosen is not None:
                    sub = jnp.where(chosen, sub, -NEG_INF)
                ps.append(jnp.exp2(s - sub).astype(BF16))
            m_ref[c] = m_new
            pss.append(ps)
        for c, (b, hh) in enumerate(chains):
            acc = alphas[c] * acc_ref[c]
            for (n, kind), p in zip(blocks, pss[c]):
                v1 = jnp.concatenate([vT_ref[b, n, hh * HEAD_DIM:(hh + 1) * HEAD_DIM, :], ones], axis=0)
                acc = acc + jnp.dot(v1, p, preferred_element_type=F32)
            acc_ref[c] = acc

    def attend_fixed(ns):
        zcs = [[jnp.dot(k_ref[b, pl.ds(pl.multiple_of(n * BLK, BLK), BLK), :], q2_ref[c],
                        preferred_element_type=F32) for n in ns]
               for c, (b, hh) in enumerate(chains)]
        pss = []
        for c, (b, hh) in enumerate(chains):
            ref = m_ref[c] - far[hh]
            ps = []
            for n, z in zip(ns, zcs[c]):
                chosen = sel_ref[c, pl.ds(n, 1), :] > 0.0
                ps.append(jnp.exp2(z - jnp.where(chosen, ref, -NEG_INF)).astype(BF16))
            pss.append(ps)
        for c, (b, hh) in enumerate(chains):
            acc = acc_ref[c]
            for n, p in zip(ns, pss[c]):
                v1 = jnp.concatenate([vT_ref[b, n, hh * HEAD_DIM:(hh + 1) * HEAD_DIM, :], ones], axis=0)
                acc = acc + jnp.dot(v1, p, preferred_element_type=F32)
            acc_ref[c] = acc

    attend([(own, "own"), (jnp.maximum(own - 1, 0), "prev")])
    n_far = jnp.maximum(own - 1, 0)

    worst = jnp.zeros((1, BLK), F32)
    for c, (b, hh) in enumerate(chains):
        q = q2_ref[c].astype(F32)
        qnorm = jnp.sqrt(jnp.sum(q * q, axis=0, keepdims=True))
        bound = knorm_ref[c] * qnorm * NORM_SLACK + far[hh]
        worst = jnp.maximum(worst, jnp.where(bound <= m_ref[c] + FIXED_MAX_MARGIN, 0.0, 1.0))
    fixed_ok = jnp.max(worst) == 0.0

    def far_loops(step, group, tail_groups):
        n_big = n_far // group
        left = n_far - n_big * group

        def body(j, carry):
            step([group * j + u for u in range(group)])
            return carry

        lax.fori_loop(0, n_big, body, 0)
        for size in tail_groups:
            @pl.when(left // size % 2 == 1)
            def _(size=size):
                first = n_far - left % (2 * size)
                step([first + u for u in range(size)])

    @pl.when(fixed_ok)
    def _():
        far_loops(attend_fixed, MOBA_FIXED_GROUP, MOBA_FIXED_TAILS)

    @pl.when(jnp.logical_not(fixed_ok))
    def _():
        far_loops(lambda ns: attend([(n, "far") for n in ns]), MOBA_GROUP, MOBA_TAILS)

    for c, (b, hh) in enumerate(chains):
        acc = acc_ref[c]
        oT_ref[b, hh * HEAD_DIM:(hh + 1) * HEAD_DIM, :] = (
            acc[:HEAD_DIM] / acc[HEAD_DIM:HEAD_DIM + 1]).astype(BF16)


def _moba(qT, k, vT, bias, far, *, gate_scale):
    B, nkb, width, _ = qT.shape
    S = nkb * BLK
    nb = min(ATTN_BATCH, B)
    assert B % nb == 0
    q_spec, k_spec, v_spec, o_spec = _attn_specs(nb, S, nkb, MOBA_HEADS, extra_args=1)
    nc = MOBA_HEADS * nb
    PAIR = MOBA_HEADS * HEAD_DIM
    kern = functools.partial(_moba_kernel, nb=nb, nkb=nkb, gate_scale=gate_scale)
    return pl.pallas_call(
        kern,
        grid_spec=pltpu.PrefetchScalarGridSpec(
            num_scalar_prefetch=1,
            grid=(B // nb, N_HEADS // MOBA_HEADS, nkb),
            in_specs=[q_spec, k_spec, v_spec,
                      pl.BlockSpec((MOBA_HEADS, 2, BLK, BLK), lambda g, p, q, far: (p, 0, 0, 0))],
            out_specs=o_spec,
            scratch_shapes=[pltpu.VMEM((nc, PAIR, BLK), BF16),
                            pltpu.VMEM((nb, 3 * nkb, PAIR), BF16),
                            pltpu.VMEM((nc, 1, BLK), F32),
                            pltpu.VMEM((nc, nkb, BLK), F32),
                            pltpu.VMEM((nc, HEAD_DIM + SUM_ROWS, BLK), F32),
                            pltpu.VMEM((nc, 1, BLK), F32)],
        ),
        out_shape=jax.ShapeDtypeStruct((B, width, S), BF16),
        compiler_params=pltpu.CompilerParams(
            dimension_semantics=("arbitrary", "arbitrary", "arbitrary"),
            vmem_limit_bytes=VMEM_LIMIT_BYTES),
        name="moba",
    )(far, qT, k, vT, bias)


def _out_kernel(oT_ref, gT_ref, x_ref, woT_ref, lng_ref, lnb_ref, h_ref, yT_ref):
    g = gT_ref[0].astype(F32)
    og = (oT_ref[0].astype(F32) * (g * (1.0 / (1.0 + jnp.exp(-g))))).astype(BF16)
    yT_ref[...] = jnp.dot(woT_ref[...], og, preferred_element_type=F32)
    r = DEEPNORM_ALPHA * x_ref[0] + yT_ref[...].T
    mu = jnp.mean(r, axis=-1, keepdims=True)
    d = r - mu
    var = jnp.mean(d * d, axis=-1, keepdims=True)
    h_ref[0] = d * lax.rsqrt(var + LN_EPS) * lng_ref[...] + lnb_ref[...]


def _out_proj_ln(oT, gT, x, woT, lng, lnb):
    B, S, D = x.shape
    width = oT.shape[1]
    rows = min(PROJ_ROWS, S)
    return pl.pallas_call(
        _out_kernel,
        grid=(B, S // rows),
        in_specs=[
            pl.BlockSpec((1, width, rows), lambda b, s: (b, 0, s)),
            pl.BlockSpec((1, width, rows), lambda b, s: (b, 0, s)),
            pl.BlockSpec((1, rows, D), lambda b, s: (b, s, 0)),
            pl.BlockSpec((D, width), lambda b, s: (0, 0)),
            pl.BlockSpec((1, D), lambda b, s: (0, 0)),
            pl.BlockSpec((1, D), lambda b, s: (0, 0)),
        ],
        out_specs=pl.BlockSpec((1, rows, D), lambda b, s: (b, s, 0)),
        out_shape=jax.ShapeDtypeStruct((B, S, D), F32),
        scratch_shapes=[pltpu.VMEM((D, rows), F32)],
        compiler_params=pltpu.CompilerParams(
            dimension_semantics=("arbitrary", "arbitrary"), vmem_limit_bytes=VMEM_LIMIT_BYTES),
        name="outproj_ln",
    )(oT, gT, x, woT, lng, lnb)


def _suffix_sum_matrix():
    return jnp.asarray(np.triu(np.ones((BLK, BLK), np.float32), k=1), BF16)


def kernel(x, w_in, w_out, ln_g, ln_b, rel_table):
    width = N_HEADS * HEAD_DIM
    scale = HEAD_DIM ** -0.5 * LOG2E
    us = _suffix_sum_matrix()
    bias = _bias_tiles(rel_table)
    far = rel_table[REL_BUCKETS - 1]
    h = x
    for layer in range(DEPTH):
        woT = w_out[layer].T.astype(BF16)
        k, qT, vT, gT = _project(h, w_in, layer, width=width, scale=scale)
        if layer % 2 == 0:
            oT = _stick_breaking(qT, k, vT, us)
        else:
            oT = _moba(qT, k, vT, bias, far, gate_scale=1.0 / scale)
        h = _out_proj_ln(oT, gT, h, woT, ln_g[layer][None, :], ln_b[layer][None, :])
    return h
```

```python
import functools
import math

import jax
import jax.numpy as jnp
import numpy as np
from jax import lax
from jax.experimental import pallas as pl
from jax.experimental.pallas import tpu as pltpu

N_HEADS = 16
HEAD_DIM = 64
DEPTH = 2
MOBA_BLOCK = 256
MOBA_TOPK = 3
REL_BUCKETS = 32
REL_MAX_DIST = 128
LN_EPS = 1e-5
DEEPNORM_ALPHA = (2.0 * DEPTH) ** 0.25
NEG_INF = -1e30
LOG2E = math.log2(math.e)
EXP2_CLAMP = 100.0

BLK = MOBA_BLOCK
PROJ_ROWS = 512
ATTN_BATCH = 4
SB_GROUP = 4
SB_TAIL_GROUPS = (2, 1)
SB_LAG = (2, 4)
MOBA_GROUP, MOBA_TAILS = 1, ()
MOBA_FIXED_GROUP, MOBA_FIXED_TAILS = 2, (1,)
SUM_ROWS = 16
FIXED_MAX_MARGIN = 60.0
NORM_SLACK = 1.0 + 2.0 ** -6
VMEM_LIMIT_BYTES = 56 * 1024 * 1024

F32 = jnp.float32
BF16 = jnp.bfloat16


def _proj_kernel(x_ref, w_ref, k_ref, qT_ref, vT_ref, gT_ref, q_scr, v_scr, g_scr, *, width, scale):
    xb = x_ref[0].astype(BF16)
    n_blk = xb.shape[0] // BLK

    def part(i):
        return jnp.dot(xb, w_ref[0, :, i * width:(i + 1) * width].astype(BF16),
                       preferred_element_type=F32)

    k_ref[0] = part(1).astype(BF16)
    q_scr[...] = part(0) * scale
    v_scr[...] = part(2)
    g_scr[...] = part(3)
    qT = q_scr[...].T.astype(BF16)
    vT = v_scr[...].T.astype(BF16)
    for j in range(n_blk):
        qT_ref[0, j] = qT[:, j * BLK:(j + 1) * BLK]
        vT_ref[0, j] = vT[:, j * BLK:(j + 1) * BLK]
    gT_ref[0] = g_scr[...].T.astype(BF16)


def _project(x, w_in, layer, *, width, scale):
    B, S, D = x.shape
    rows = min(PROJ_ROWS, S)
    assert S % rows == 0 and rows % BLK == 0
    n_blk = rows // BLK
    nkb = S // BLK
    kern = functools.partial(_proj_kernel, width=width, scale=scale)
    return pl.pallas_call(
        kern,
        grid=(B, S // rows),
        in_specs=[
            pl.BlockSpec((1, rows, D), lambda b, s: (b, s, 0)),
            pl.BlockSpec((1, D, 4 * width), lambda b, s: (layer, 0, 0), pipeline_mode=pl.Buffered(1)),
        ],
        out_specs=[
            pl.BlockSpec((1, rows, width), lambda b, s: (b, s, 0)),
            pl.BlockSpec((1, n_blk, width, BLK), lambda b, s: (b, s, 0, 0)),
            pl.BlockSpec((1, n_blk, width, BLK), lambda b, s: (b, s, 0, 0)),
            pl.BlockSpec((1, width, rows), lambda b, s: (b, 0, s)),
        ],
        out_shape=[
            jax.ShapeDtypeStruct((B, S, width), BF16),
            jax.ShapeDtypeStruct((B, nkb, width, BLK), BF16),
            jax.ShapeDtypeStruct((B, nkb, width, BLK), BF16),
            jax.ShapeDtypeStruct((B, width, S), BF16),
        ],
        scratch_shapes=[pltpu.VMEM((rows, width), F32)] * 3,
        compiler_params=pltpu.CompilerParams(
            dimension_semantics=("arbitrary", "arbitrary"), vmem_limit_bytes=VMEM_LIMIT_BYTES),
        name="proj",
    )(x, w_in)


SB_HEADS, MOBA_HEADS = 4, 4


def _chains(nb, hps):
    return [(b, hh) for b in range(nb) for hh in range(hps)]


def _stage_pair_queries(qT_ref, q2_ref, nb, hps):
    zeros = jnp.zeros((HEAD_DIM, BLK), BF16)
    for c, (b, hh) in enumerate(_chains(nb, hps)):
        q = qT_ref[b, 0, hh * HEAD_DIM:(hh + 1) * HEAD_DIM, :]
        q2_ref[c] = jnp.concatenate([zeros] * hh + [q] + [zeros] * (hps - 1 - hh), axis=0)


def _attn_specs(nb, S, nkb, hps, extra_args=0):
    PAIR = hps * HEAD_DIM
    def im(f):
        if extra_args:
            return lambda g, p, q, *_: f(g, p, q)
        return f
    q_spec = pl.BlockSpec((nb, 1, PAIR, BLK), im(lambda g, p, q: (g, q, p, 0)))
    k_spec = pl.BlockSpec((nb, S, PAIR), im(lambda g, p, q: (g, 0, p)))
    v_spec = pl.BlockSpec((nb, nkb, PAIR, BLK), im(lambda g, p, q: (g, 0, p, 0)))
    o_spec = pl.BlockSpec((nb, PAIR, BLK), im(lambda g, p, q: (g, p, q)))
    return q_spec, k_spec, v_spec, o_spec


def _sb_kernel(qT_ref, k_ref, vT_ref, us_ref, oT_ref, q2_ref, acc_ref, r_ref, *, nb):
    qi = pl.program_id(2)
    chains = _chains(nb, SB_HEADS)
    _stage_pair_queries(qT_ref, q2_ref, nb, SB_HEADS)
    acc_ref[...] = jnp.zeros_like(acc_ref)
    r_ref[...] = jnp.zeros_like(r_ref)

    def process(kbs, masked):
        if masked:
            row = lax.broadcasted_iota(jnp.int32, (BLK, BLK), 0)
            col = lax.broadcasted_iota(jnp.int32, (BLK, BLK), 1)
            past = row < col
        tiles = [(kb, c, b, hh) for kb in kbs for c, (b, hh) in enumerate(chains)]
        n_tiles = len(tiles)
        zs, ys, ps, css, pvs = {}, {}, {}, {}, {}
        lag_p, lag_w = SB_LAG
        for step in range(n_tiles + lag_w):
            if step < n_tiles:
                kb, c, b, hh = tiles[step]
                zs[step] = jnp.dot(k_ref[b, pl.ds(pl.multiple_of(kb * BLK, BLK), BLK), :], q2_ref[c],
                                   preferred_element_type=F32)
            i = step - lag_p
            if 0 <= i < n_tiles:
                z = zs.pop(i)
                p = jnp.maximum(jnp.log(1.0 + jnp.exp2(jnp.minimum(z, EXP2_CLAMP))) * LOG2E, z)
                if masked:
                    p = jnp.where(past, p, 0.0)
                ys[i] = z - p
                ps[i] = p.astype(BF16)
                css[i] = jnp.dot(us_ref[...], ps[i], preferred_element_type=F32)
            j = step - lag_w
            if 0 <= j < n_tiles:
                kb, c, b, hh = tiles[j]
                w = jnp.exp2(ys.pop(j) - css[j])
                if masked:
                    w = jnp.where(past, w, 0.0)
                pvs[j] = jnp.dot(vT_ref[b, kb, hh * HEAD_DIM:(hh + 1) * HEAD_DIM, :], w.astype(BF16),
                                 preferred_element_type=F32)
        for c in range(len(chains)):
            r = r_ref[c]
            acc = acc_ref[c]
            for t, (kb, tc, b, hh) in enumerate(tiles):
                if tc == c:
                    acc = acc + pvs[t] * jnp.exp2(-r)
                    r = r + css[t][0:1, :] + ps[t][0:1, :].astype(F32)
            acc_ref[c] = acc
            r_ref[c] = r

    process([qi], True)
    n_big = qi // SB_GROUP
    left = qi - n_big * SB_GROUP

    def group(j, carry):
        first = qi - 1 - SB_GROUP * j
        process([first - u for u in range(SB_GROUP)], False)
        return carry

    lax.fori_loop(0, n_big, group, 0)
    for size in SB_TAIL_GROUPS:
        @pl.when(left // size % 2 == 1)
        def _(size=size):
            first = left % (2 * size) - 1
            process([first - u for u in range(size)], False)
    for c, (b, hh) in enumerate(chains):
        oT_ref[b, hh * HEAD_DIM:(hh + 1) * HEAD_DIM, :] = acc_ref[c].astype(BF16)


def _stick_breaking(qT, k, vT, us):
    B, nkb, width, _ = qT.shape
    S = nkb * BLK
    nb = min(ATTN_BATCH, B)
    assert B % nb == 0
    q_spec, k_spec, v_spec, o_spec = _attn_specs(nb, S, nkb, SB_HEADS)
    nc = SB_HEADS * nb
    PAIR = SB_HEADS * HEAD_DIM
    return pl.pallas_call(
        functools.partial(_sb_kernel, nb=nb),
        grid=(B // nb, N_HEADS // SB_HEADS, nkb),
        in_specs=[q_spec, k_spec, v_spec, pl.BlockSpec((BLK, BLK), lambda g, p, q: (0, 0))],
        out_specs=o_spec,
        out_shape=jax.ShapeDtypeStruct((B, width, S), BF16),
        scratch_shapes=[pltpu.VMEM((nc, PAIR, BLK), BF16),
                        pltpu.VMEM((nc, HEAD_DIM, BLK), F32),
                        pltpu.VMEM((nc, 1, BLK), F32)],
        compiler_params=pltpu.CompilerParams(
            dimension_semantics=("arbitrary", "arbitrary", "arbitrary"),
            vmem_limit_bytes=VMEM_LIMIT_BYTES),
        name="stickbreak",
    )(qT, k, vT, us)


def _bias_kernel(table_ref, bucket_ref, bias_ref):
    h = pl.program_id(0)
    bk = bucket_ref[...]
    by_dist = jnp.full(bk.shape, NEG_INF, F32)
    for i in range(REL_BUCKETS):
        by_dist = jnp.where(bk == i, table_ref[i * N_HEADS + h] * LOG2E, by_dist)
    rows = jnp.broadcast_to(by_dist, (BLK, 4 * BLK))
    skew = pltpu.roll(rows, 0, 1, stride=1, stride_axis=0)
    bias_ref[0, 0] = skew[:, BLK:2 * BLK]
    bias_ref[0, 1] = skew[:, 2 * BLK:3 * BLK]


def _t5_bucket_np(dist):
    n = np.maximum(dist, 0)
    max_exact = REL_BUCKETS // 2
    nf = np.maximum(n, 1).astype(np.float64)
    large = max_exact + (np.log(nf / max_exact) / math.log(REL_MAX_DIST / max_exact)
                         * (REL_BUCKETS - max_exact)).astype(np.int32)
    large = np.minimum(large, REL_BUCKETS - 1)
    return np.where(n < max_exact, n, large).astype(np.int32)


def _bias_tiles(rel_table):
    dist = np.arange(4 * BLK) - BLK
    buckets = np.where(dist >= 0, _t5_bucket_np(dist), -1).astype(np.int32)[None, :]
    return pl.pallas_call(
        _bias_kernel,
        grid_spec=pltpu.PrefetchScalarGridSpec(
            num_scalar_prefetch=1,
            grid=(N_HEADS,),
            in_specs=[pl.BlockSpec((1, 4 * BLK), lambda h, tab: (0, 0))],
            out_specs=pl.BlockSpec((1, 2, BLK, BLK), lambda h, tab: (h, 0, 0, 0)),
        ),
        out_shape=jax.ShapeDtypeStruct((N_HEADS, 2, BLK, BLK), F32),
        name="moba_bias",
    )(rel_table.reshape(-1), jnp.asarray(buckets))


def _moba_kernel(far_ref, qT_ref, k_ref, vT_ref, bias_ref, oT_ref,
                 q2_ref, kmean_ref, knorm_ref, sel_ref, acc_ref, m_ref, *, nb, nkb, gate_scale):
    pair = pl.program_id(1)
    own = pl.program_id(2)
    chains = _chains(nb, MOBA_HEADS)
    PAIR = MOBA_HEADS * HEAD_DIM
    _stage_pair_queries(qT_ref, q2_ref, nb, MOBA_HEADS)

    @pl.when(own == 0)
    def _():
        for b in range(nb):
            for n in range(nkb):
                kb = k_ref[b, n * BLK:(n + 1) * BLK, :].astype(F32)
                km = jnp.mean(kb, axis=0, keepdims=True)
                for part in range(3):
                    piece = km.astype(BF16)
                    kmean_ref[b, part * nkb + n:part * nkb + n + 1, :] = piece
                    km = km - piece.astype(F32)
        lane = lax.broadcasted_iota(jnp.int32, (PAIR, PAIR), 0) // HEAD_DIM
        same_head = (lane == lax.broadcasted_iota(jnp.int32, (PAIR, PAIR), 1) // HEAD_DIM).astype(BF16)
        for b in range(nb):
            kf = k_ref[b].astype(F32)
            norm2 = jnp.max(jnp.dot((kf * kf).astype(BF16), same_head, preferred_element_type=F32),
                            axis=0, keepdims=True)
            for hh in range(MOBA_HEADS):
                knorm_ref[MOBA_HEADS * b + hh] = jnp.broadcast_to(
                    jnp.sqrt(norm2[:, hh * HEAD_DIM:hh * HEAD_DIM + 1]), (1, BLK))

    n_ok = jnp.minimum(own, MOBA_TOPK)
    for c, (b, hh) in enumerate(chains):
        parts = jnp.dot(kmean_ref[b], q2_ref[c], preferred_element_type=F32)
        gate = (parts[:nkb] + parts[nkb:2 * nkb] + parts[2 * nkb:]) * gate_scale
        blk = lax.broadcasted_iota(jnp.int32, gate.shape, 0)
        g = jnp.where(blk < own, gate, NEG_INF)
        sel = jnp.zeros(gate.shape, F32)
        for j in range(min(MOBA_TOPK, nkb)):
            top = jnp.max(g, axis=0, keepdims=True)
            idx = jnp.min(jnp.where(g == top, blk, nkb), axis=0, keepdims=True)
            pick = blk == idx
            sel = jnp.where(pick & (j < n_ok), 1.0, sel)
            g = jnp.where(pick, -jnp.inf, g)
        sel_ref[c] = sel

    m_ref[...] = jnp.full(m_ref.shape, NEG_INF, F32)
    acc_ref[...] = jnp.zeros_like(acc_ref)
    far = [far_ref[MOBA_HEADS * pair + hh] * LOG2E for hh in range(MOBA_HEADS)]
    ones = jnp.ones((SUM_ROWS, BLK), BF16)

    def attend(blocks):
        zcs = [[jnp.dot(k_ref[b, pl.ds(pl.multiple_of(n * BLK, BLK), BLK), :], q2_ref[c],
                        preferred_element_type=F32) for n, kind in blocks]
               for c, (b, hh) in enumerate(chains)]
        pss, alphas = [], []
        for c, (b, hh) in enumerate(chains):
            m_old = m_ref[c]
            m_new = m_old
            terms = []
            for (n, kind), z in zip(blocks, zcs[c]):
                if kind == "own":
                    s, shift, chosen = z + bias_ref[hh, 0], 0.0, None
                elif kind == "prev":
                    s, shift, chosen = z + bias_ref[hh, 1], 0.0, sel_ref[c, pl.ds(n, 1), :] > 0.0
                else:
                    s, shift, chosen = z, far[hh], sel_ref[c, pl.ds(n, 1), :] > 0.0
                m_blk = jnp.max(s, axis=0, keepdims=True) + shift
                if chosen is not None:
                    m_blk = jnp.where(chosen, m_blk, NEG_INF)
                m_new = jnp.maximum(m_new, m_blk)
                terms.append((s, shift, chosen))
            alphas.append(jnp.exp2(m_old - m_new))
            ps = []
            for s, shift, chosen in terms:
                sub = m_new - shift
                if chosen is not None:
                    sub = jnp.where(chosen, sub, -NEG_INF)
                ps.append(jnp.exp2(s - sub).astype(BF16))
            m_ref[c] = m_new
            pss.append(ps)
        for c, (b, hh) in enumerate(chains):
            acc = alphas[c] * acc_ref[c]
            for (n, kind), p in zip(blocks, pss[c]):
                v1 = jnp.concatenate([vT_ref[b, n, hh * HEAD_DIM:(hh + 1) * HEAD_DIM, :], ones], axis=0)
                acc = acc + jnp.dot(v1, p, preferred_element_type=F32)
            acc_ref[c] = acc

    def attend_fixed(ns):
        zcs = [[jnp.dot(k_ref[b, pl.ds(pl.multiple_of(n * BLK, BLK), BLK), :], q2_ref[c],
                        preferred_element_type=F32) for n in ns]
               for c, (b, hh) in enumerate(chains)]
        pss = []
        for c, (b, hh) in enumerate(chains):
            ref = m_ref[c] - far[hh]
            ps = []
            for n, z in zip(ns, zcs[c]):
                chosen = sel_ref[c, pl.ds(n, 1), :] > 0.0
                ps.append(jnp.exp2(z - jnp.where(chosen, ref, -NEG_INF)).astype(BF16))
            pss.append(ps)
        for c, (b, hh) in enumerate(chains):
            acc = acc_ref[c]
            for n, p in zip(ns, pss[c]):
                v1 = jnp.concatenate([vT_ref[b, n, hh * HEAD_DIM:(hh + 1) * HEAD_DIM, :], ones], axis=0)
                acc = acc + jnp.dot(v1, p, preferred_element_type=F32)
            acc_ref[c] = acc

    attend([(own, "own")])
    attend([(jnp.maximum(own - 1, 0), "prev")])
    n_far = jnp.maximum(own - 1, 0)

    worst = jnp.zeros((1, BLK), F32)
    for c, (b, hh) in enumerate(chains):
        q = q2_ref[c].astype(F32)
        qnorm = jnp.sqrt(jnp.sum(q * q, axis=0, keepdims=True))
        bound = knorm_ref[c] * qnorm * NORM_SLACK + far[hh]
        worst = jnp.maximum(worst, jnp.where(bound <= m_ref[c] + FIXED_MAX_MARGIN, 0.0, 1.0))
    fixed_ok = jnp.max(worst) == 0.0

    def far_loops(step, group, tail_groups):
        n_big = n_far // group
        left = n_far - n_big * group

        def body(j, carry):
            step([group * j + u for u in range(group)])
            return carry

        lax.fori_loop(0, n_big, body, 0)
        for size in tail_groups:
            @pl.when(left // size % 2 == 1)
            def _(size=size):
                first = n_far - left % (2 * size)
                step([first + u for u in range(size)])

    @pl.when(fixed_ok)
    def _():
        far_loops(attend_fixed, MOBA_FIXED_GROUP, MOBA_FIXED_TAILS)

    @pl.when(jnp.logical_not(fixed_ok))
    def _():
        far_loops(lambda ns: attend([(n, "far") for n in ns]), MOBA_GROUP, MOBA_TAILS)

    for c, (b, hh) in enumerate(chains):
        acc = acc_ref[c]
        oT_ref[b, hh * HEAD_DIM:(hh + 1) * HEAD_DIM, :] = (
            acc[:HEAD_DIM] / acc[HEAD_DIM:HEAD_DIM + 1]).astype(BF16)


def _moba(qT, k, vT, bias, far, *, gate_scale):
    B, nkb, width, _ = qT.shape
    S = nkb * BLK
    nb = min(ATTN_BATCH, B)
    assert B % nb == 0
    q_spec, k_spec, v_spec, o_spec = _attn_specs(nb, S, nkb, MOBA_HEADS, extra_args=1)
    nc = MOBA_HEADS * nb
    PAIR = MOBA_HEADS * HEAD_DIM
    kern = functools.partial(_moba_kernel, nb=nb, nkb=nkb, gate_scale=gate_scale)
    return pl.pallas_call(
        kern,
        grid_spec=pltpu.PrefetchScalarGridSpec(
            num_scalar_prefetch=1,
            grid=(B // nb, N_HEADS // MOBA_HEADS, nkb),
            in_specs=[q_spec, k_spec, v_spec,
                      pl.BlockSpec((MOBA_HEADS, 2, BLK, BLK), lambda g, p, q, far: (p, 0, 0, 0))],
            out_specs=o_spec,
            scratch_shapes=[pltpu.VMEM((nc, PAIR, BLK), BF16),
                            pltpu.VMEM((nb, 3 * nkb, PAIR), BF16),
                            pltpu.VMEM((nc, 1, BLK), F32),
                            pltpu.VMEM((nc, nkb, BLK), F32),
                            pltpu.VMEM((nc, HEAD_DIM + SUM_ROWS, BLK), F32),
                            pltpu.VMEM((nc, 1, BLK), F32)],
        ),
        out_shape=jax.ShapeDtypeStruct((B, width, S), BF16),
        compiler_params=pltpu.CompilerParams(
            dimension_semantics=("arbitrary", "arbitrary", "arbitrary"),
            vmem_limit_bytes=VMEM_LIMIT_BYTES),
        name="moba",
    )(far, qT, k, vT, bias)


def _out_kernel(oT_ref, gT_ref, x_ref, woT_ref, lng_ref, lnb_ref, h_ref, yT_ref):
    g = gT_ref[0].astype(F32)
    og = (oT_ref[0].astype(F32) * (g * (1.0 / (1.0 + jnp.exp(-g))))).astype(BF16)
    yT_ref[...] = jnp.dot(woT_ref[...], og, preferred_element_type=F32)
    r = DEEPNORM_ALPHA * x_ref[0] + yT_ref[...].T
    mu = jnp.mean(r, axis=-1, keepdims=True)
    d = r - mu
    var = jnp.mean(d * d, axis=-1, keepdims=True)
    h_ref[0] = d * lax.rsqrt(var + LN_EPS) * lng_ref[...] + lnb_ref[...]


def _out_proj_ln(oT, gT, x, woT, lng, lnb):
    B, S, D = x.shape
    width = oT.shape[1]
    rows = min(PROJ_ROWS, S)
    return pl.pallas_call(
        _out_kernel,
        grid=(B, S // rows),
        in_specs=[
            pl.BlockSpec((1, width, rows), lambda b, s: (b, 0, s)),
            pl.BlockSpec((1, width, rows), lambda b, s: (b, 0, s)),
            pl.BlockSpec((1, rows, D), lambda b, s: (b, s, 0)),
            pl.BlockSpec((D, width), lambda b, s: (0, 0)),
            pl.BlockSpec((1, D), lambda b, s: (0, 0)),
            pl.BlockSpec((1, D), lambda b, s: (0, 0)),
        ],
        out_specs=pl.BlockSpec((1, rows, D), lambda b, s: (b, s, 0)),
        out_shape=jax.ShapeDtypeStruct((B, S, D), F32),
        scratch_shapes=[pltpu.VMEM((D, rows), F32)],
        compiler_params=pltpu.CompilerParams(
            dimension_semantics=("arbitrary", "arbitrary"), vmem_limit_bytes=VMEM_LIMIT_BYTES),
        name="outproj_ln",
    )(oT, gT, x, woT, lng, lnb)


def _suffix_sum_matrix():
    return jnp.asarray(np.triu(np.ones((BLK, BLK), np.float32), k=1), BF16)


def kernel(x, w_in, w_out, ln_g, ln_b, rel_table):
    width = N_HEADS * HEAD_DIM
    scale = HEAD_DIM ** -0.5 * LOG2E
    us = _suffix_sum_matrix()
    bias = _bias_tiles(rel_table)
    far = rel_table[REL_BUCKETS - 1]
    h = x
    for layer in range(DEPTH):
        woT = w_out[layer].T.astype(BF16)
        k, qT, vT, gT = _project(h, w_in, layer, width=width, scale=scale)
        if layer % 2 == 0:
            oT = _stick_breaking(qT, k, vT, us)
        else:
            oT = _moba(qT, k, vT, bias, far, gate_scale=1.0 / scale)
        h = _out_proj_ln(oT, gT, h, woT, ln_g[layer][None, :], ln_b[layer][None, :])
    return h
```
